```python
import math
import jax
import jax.numpy as jnp
from jax import lax
import numpy as np

D_MODEL = 2048
BATCH = 1
SEQ = 8192
DEPTH = 1
DEC_BATCH = 32
DEC_SEQ = 8
PAST_LEN = 16384
PAGE_SIZE = 128

N_META = 16
BLOCK = 128
META_PAD = (-N_META) % BLOCK
D_MIX = D_MODEL
D_DIFF = D_MIX // 2
D_MLSTM = D_MIX - D_DIFF
DA_HEADS = 8
DA_DH = D_DIFF // (2 * DA_HEADS)
DA_DV = 2 * DA_DH
ML_HEADS = 4
ML_DV = D_MLSTM // ML_HEADS
ML_DQK = ML_DV // 2
D_IN = 3 * D_DIFF + 2 * ML_HEADS * ML_DQK + 2 * D_MLSTM + 2 * ML_HEADS
N_BUCKETS = 32
MAX_DISTANCE = 128
N_GROUPS = 4
EXPERTS_PER_GROUP = 8
N_EXPERTS = N_GROUPS * EXPERTS_PER_GROUP
TOP_K = 2
D_FF = D_MODEL // 4
MOE_BLOCK = 128
ALPHA = (2.0 * DEPTH) ** 0.25
BETA = (8.0 * DEPTH) ** -0.25
LN_EPS = 1e-5
NEG = -1e30

kernel_name = 'hymba_diffattn_mlstm_hmoe_step'


def layer_norm(x, g, b):
    xf = x.astype(jnp.float32)
    mu = jnp.mean(xf, axis=-1, keepdims=True)
    var = jnp.mean(jnp.square(xf - mu), axis=-1, keepdims=True)
    return ((xf - mu) * lax.rsqrt(var + LN_EPS) * g.astype(jnp.float32) + b.astype(jnp.float32)).astype(x.dtype)


def rms_norm(x, g):
    xf = x.astype(jnp.float32)
    return (xf * lax.rsqrt(jnp.mean(xf * xf, axis=-1, keepdims=True) + LN_EPS) * g.astype(jnp.float32)).astype(x.dtype)


def t5_bucket(rel):
    n = jnp.maximum(rel, 0)
    max_exact = N_BUCKETS // 2
    nf = jnp.maximum(n, max_exact).astype(jnp.float32)
    large = max_exact + (jnp.log(nf / max_exact) / math.log(MAX_DISTANCE / max_exact) * (N_BUCKETS - max_exact)).astype(jnp.int32)
    large = jnp.minimum(large, N_BUCKETS - 1)
    return jnp.where(n < max_exact, n, large)


def diff_attention(q, k, v, q_pos, k_pos, table, lam):
    s = jnp.einsum('qhcd,khcd->hcqk', q.astype(jnp.float32), k.astype(jnp.float32)) * DA_DH ** -0.5
    bias = jnp.moveaxis(table.astype(jnp.float32)[t5_bucket(q_pos[:, None] - k_pos[None, :])], -1, 0)
    s = s + bias[:, None]
    mask = (k_pos[None, :] >= 0) & (k_pos[None, :] <= q_pos[:, None])
    p = jax.nn.softmax(jnp.where(mask, s, NEG), axis=-1)
    a = p[:, 0] - lam * p[:, 1]
    return jnp.einsum('hqk,khe->qhe', a, v.astype(jnp.float32))


def prompt_diff_attention(q, k, v, table, lam):
    b, lp = q.shape[0], q.shape[1]
    nb = lp // BLOCK
    pos = jnp.arange(lp, dtype=jnp.int32) - META_PAD
    q_blocks = jnp.moveaxis(q.reshape((b, nb, BLOCK) + q.shape[2:]), 1, 0)
    pos_blocks = pos.reshape(nb, BLOCK)

    def one_block(args):
        q_blk, q_pos = args
        return jax.vmap(lambda qq, kk, vv: diff_attention(qq, kk, vv, q_pos, pos, table, lam))(q_blk, k, v)

    out = lax.map(one_block, (q_blocks, pos_blocks))
    return jnp.moveaxis(out, 0, 1).reshape(b, lp, DA_HEADS, DA_DV)


def sample_diff_attention(q, k_new, v_new, cache_k, cache_v, page_table, table, lam):
    past = page_table.shape[1] * cache_k.shape[1]
    ds = q.shape[1]
    k_pos = jnp.arange(past + ds, dtype=jnp.int32)
    q_pos = past + jnp.arange(ds, dtype=jnp.int32)

    def one_seq(args):
        qq, kn, vn, pages = args
        kp = cache_k[pages].reshape(past, DA_HEADS, 2, DA_DH).astype(kn.dtype)
        vp = cache_v[pages].reshape(past, DA_HEADS, DA_DV).astype(vn.dtype)
        kk = jnp.concatenate([kp, kn], axis=0)
        vv = jnp.concatenate([vp, vn], axis=0)
        return diff_attention(qq, kk, vv, q_pos, k_pos, table, lam)

    return lax.map(one_seq, (q, k_new, v_new, page_table))


def mlstm_chunk(carry, xs):
    C, n, m = carry
    q, k, v, logi, logf = xs
    L = q.shape[1]
    b = jnp.cumsum(logf, axis=1)
    d = b[:, :, None, :] - b[:, None, :, :] + logi[:, None, :, :]
    causal = jnp.tril(jnp.ones((L, L), dtype=bool))[None, :, :, None]
    d = jnp.where(causal, d, -jnp.inf)
    m_inter = m[:, None, :] + b
    m_t = jnp.maximum(m_inter, jnp.max(d, axis=2))
    w = jnp.exp(d - m_t[:, :, None, :])
    s = w * jnp.einsum('bthd,bshd->btsh', q, k)
    inter = jnp.exp(m_inter - m_t)
    num = jnp.einsum('btsh,bshe->bthe', s, v) + inter[..., None] * jnp.einsum('bthd,bhde->bthe', q, C)
    den = jnp.sum(s, axis=2) + inter * jnp.einsum('bthd,bhd->bth', q, n)
    h = num / jnp.maximum(jnp.abs(den), jnp.exp(-m_t))[..., None]
    m_new = m_t[:, -1]
    w_end = jnp.exp(b[:, -1:, :] - b + logi - m_new[:, None, :])
    decay = jnp.exp(m_inter[:, -1] - m_new)
    C_new = decay[:, :, None, None] * C + jnp.einsum('bsh,bshd,bshe->bhde', w_end, k, v)
    n_new = decay[:, :, None] * n + jnp.einsum('bsh,bshd->bhd', w_end, k)
    return (C_new, n_new, m_new), h


def mlstm_prompt(q, k, v, logi, logf):
    b, lp = q.shape[0], q.shape[1]
    nb = lp // BLOCK

    def to_chunks(t):
        return jnp.moveaxis(t.reshape((b, nb, BLOCK) + t.shape[2:]), 1, 0)

    init = (jnp.zeros((b, ML_HEADS, ML_DQK, ML_DV), jnp.float32),
            jnp.zeros((b, ML_HEADS, ML_DQK), jnp.float32),
            jnp.zeros((b, ML_HEADS), jnp.float32))
    carry, h = lax.scan(mlstm_chunk, init, (to_chunks(q), to_chunks(k), to_chunks(v), to_chunks(logi), to_chunks(logf)))
    return jnp.moveaxis(h, 0, 1).reshape(b, lp, ML_HEADS, ML_DV), carry


def hier_moe(x, w_rg, b_rg, w_re, b_re, w_gate, w_up, w_down):
    T = x.shape[0]
    xf = x.astype(jnp.float32)
    g_logits = xf @ w_rg.astype(jnp.float32) + b_rg.astype(jnp.float32)
    p_group = jax.nn.softmax(g_logits, axis=-1)
    g_sel = jnp.argmax(g_logits, axis=-1)
    e_logits = (xf @ w_re.astype(jnp.float32) + b_re.astype(jnp.float32)).reshape(T, N_GROUPS, EXPERTS_PER_GROUP)
    e_logits = jnp.take_along_axis(e_logits, g_sel[:, None, None], axis=1)[:, 0]
    top_val, top_idx = lax.top_k(e_logits, TOP_K)
    gates = jax.nn.softmax(top_val, axis=-1) * jnp.take_along_axis(p_group, g_sel[:, None], axis=1)
    expert = g_sel[:, None] * EXPERTS_PER_GROUP + top_idx
    n_assign = T * TOP_K
    n_blocks = -(-n_assign // MOE_BLOCK) + N_EXPERTS
    e_flat = expert.reshape(n_assign)
    tok_flat = jnp.repeat(jnp.arange(T, dtype=jnp.int32), TOP_K)
    gate_flat = gates.reshape(n_assign)
    order = jnp.argsort(e_flat)
    e_sorted = e_flat[order]
    counts = jax.ops.segment_sum(jnp.ones((n_assign,), jnp.int32), e_flat, num_segments=N_EXPERTS)
    padded = (counts + MOE_BLOCK - 1) // MOE_BLOCK * MOE_BLOCK
    start = jnp.cumsum(counts) - counts
    pend = jnp.cumsum(padded)
    pstart = pend - padded
    dest = pstart[e_sorted] + (jnp.arange(n_assign, dtype=jnp.int32) - start[e_sorted])
    n_slots = n_blocks * MOE_BLOCK
    buf_tok = jnp.zeros((n_slots,), jnp.int32).at[dest].set(tok_flat[order])
    buf_gate = jnp.zeros((n_slots,), jnp.float32).at[dest].set(gate_flat[order])
    block_start = jnp.arange(n_blocks, dtype=jnp.int32) * MOE_BLOCK
    block_expert = jnp.minimum(jnp.sum(block_start[:, None] >= pend[None, :], axis=1), N_EXPERTS - 1)
    xb = x[buf_tok].reshape(n_blocks, MOE_BLOCK, D_MODEL)

    def expert_block(args):
        xi, e = args
        hh = jax.nn.silu(xi @ w_gate[e]) * (xi @ w_up[e])
        return hh @ w_down[e]

    yb = lax.map(expert_block, (xb, block_expert)).reshape(n_slots, D_MODEL)
    y = jnp.zeros((T, D_MODEL), jnp.float32).at[buf_tok].add(yb.astype(jnp.float32) * buf_gate[:, None])
    return y.astype(x.dtype)


def pad_front(t, value=0.0):
    return jnp.pad(t, [(0, 0), (META_PAD, 0)] + [(0, 0)] * (t.ndim - 2), constant_values=value)


def setup_inputs(seed: int = 0) -> dict:
    key = jax.random.key(seed)
    ks = jax.random.split(key, 34)
    f32 = jnp.float32
    n_pages = PAST_LEN // PAGE_SIZE
    n_phys = (DEC_BATCH * n_pages * 5) // 4

    def nrm(k, shape, s=1.0):
        return jax.random.normal(k, shape, f32) * s

    col_scale = jnp.concatenate([
        jnp.ones((2 * D_DIFF,), f32), jnp.full((D_DIFF,), BETA, f32),
        jnp.ones((2 * ML_HEADS * ML_DQK,), f32), jnp.full((D_MLSTM,), BETA, f32),
        jnp.ones((D_MLSTM + 2 * ML_HEADS,), f32)])
    b_gates = jnp.concatenate([
        nrm(ks[13], (DEPTH, ML_HEADS), 0.1),
        jnp.broadcast_to(jnp.linspace(3.0, 6.0, ML_HEADS, dtype=f32), (DEPTH, ML_HEADS)) + nrm(ks[14], (DEPTH, ML_HEADS), 0.1)], axis=1)
    page_table = jax.random.permutation(ks[7], n_phys)[:DEC_BATCH * n_pages].reshape(DEC_BATCH, n_pages).astype(jnp.int32)
    return {
        'x_prompt': nrm(ks[0], (BATCH, SEQ, D_MODEL)),
        'x_sample': nrm(ks[1], (DEC_BATCH, DEC_SEQ, D_MODEL)),
        'cache_k': nrm(ks[2], (DEPTH, n_phys, PAGE_SIZE, DA_HEADS, 2 * DA_DH)),
        'cache_v': nrm(ks[3], (DEPTH, n_phys, PAGE_SIZE, DA_HEADS, DA_DV)),
        'state_C': nrm(ks[4], (DEPTH, DEC_BATCH, ML_HEADS, ML_DQK, ML_DV), 0.5),
        'state_n': nrm(ks[5], (DEPTH, DEC_BATCH, ML_HEADS, ML_DQK), 0.5),
        'state_m': nrm(ks[6], (DEPTH, DEC_BATCH, ML_HEADS)),
        'page_table': page_table,
        'meta_tokens': nrm(ks[8], (N_META, D_MODEL)),
        'ln_in_g': 1.0 + nrm(ks[9], (D_MODEL,), 0.05),
        'ln_in_b': nrm(ks[10], (D_MODEL,), 0.02),
        'rel_bias': nrm(ks[11], (N_BUCKETS, DA_HEADS), 0.5),
        'w_in': nrm(ks[12], (DEPTH, D_MODEL, D_IN), D_MODEL ** -0.5) * col_scale,
        'b_gates': b_gates,
        'lambda_q1': nrm(ks[15], (DEPTH, DA_DH), 0.1),
        'lambda_k1': nrm(ks[16], (DEPTH, DA_DH), 0.1),
        'lambda_q2': nrm(ks[17], (DEPTH, DA_DH), 0.1),
        'lambda_k2': nrm(ks[18], (DEPTH, DA_DH), 0.1),
        'diff_norm_g': 1.0 + nrm(ks[19], (DEPTH, DA_DV), 0.05),
        'mlstm_norm_g': 1.0 + nrm(ks[20], (DEPTH, D_MLSTM), 0.05),
        'w_out': nrm(ks[21], (DEPTH, D_MIX, D_MODEL), BETA * D_MIX ** -0.5),
        'ln_mix_g': 1.0 + nrm(ks[22], (DEPTH, D_MODEL), 0.05),
        'ln_mix_b': nrm(ks[23], (DEPTH, D_MODEL), 0.02),
        'w_router_g': nrm(ks[24], (DEPTH, D_MODEL, N_GROUPS), D_MODEL ** -0.5),
        'b_router_g': nrm(ks[25], (DEPTH, N_GROUPS), 0.01),
        'w_router_e': nrm(ks[26], (DEPTH, D_MODEL, N_EXPERTS), D_MODEL ** -0.5),
        'b_router_e': nrm(ks[27], (DEPTH, N_EXPERTS), 0.01),
        'w_gate': nrm(ks[28], (DEPTH, N_EXPERTS, D_MODEL, D_FF), D_MODEL ** -0.5),
        'w_up': nrm(ks[29], (DEPTH, N_EXPERTS, D_MODEL, D_FF), D_MODEL ** -0.5),
        'w_down': nrm(ks[30], (DEPTH, N_EXPERTS, D_FF, D_MODEL), BETA * D_FF ** -0.5),
        'ln_ffn_g': 1.0 + nrm(ks[31], (DEPTH, D_MODEL), 0.05),
        'ln_ffn_b': nrm(ks[32], (DEPTH, D_MODEL), 0.02),
    }


def reference(x_prompt, x_sample, cache_k, cache_v, state_C, state_n, state_m, page_table,
              meta_tokens, ln_in_g, ln_in_b, rel_bias, w_in, b_gates,
              lambda_q1, lambda_k1, lambda_q2, lambda_k2, diff_norm_g, mlstm_norm_g, w_out,
              ln_mix_g, ln_mix_b, w_router_g, b_router_g, w_router_e, b_router_e,
              w_gate, w_up, w_down, ln_ffn_g, ln_ffn_b):
    f32 = jnp.float32
    dt = x_prompt.dtype
    B, S = x_prompt.shape[0], x_prompt.shape[1]
    DB, DS = x_sample.shape[0], x_sample.shape[1]
    LQ = N_META + S
    P = B * LQ
    meta = jnp.broadcast_to(meta_tokens.astype(dt)[None], (B, N_META, D_MODEL))
    xp = jnp.concatenate([meta, x_prompt], axis=1)
    h = jnp.concatenate([xp.reshape(P, D_MODEL), x_sample.astype(dt).reshape(DB * DS, D_MODEL)], axis=0)
    h = layer_norm(h, ln_in_g, ln_in_b)
    sizes = [D_DIFF] * 3 + [ML_HEADS * ML_DQK] * 2 + [D_MLSTM] * 2 + [ML_HEADS] * 2
    split_idx = np.cumsum(sizes)[:-1].tolist()

    def prompt_part(t):
        return t[:P].reshape((B, LQ) + t.shape[1:])

    def sample_part(t):
        return t[P:].reshape((DB, DS) + t.shape[1:])

    kp_l, vp_l, Cp_l, np_l, mp_l = [], [], [], [], []
    ks_l, vs_l, Cs_l, ns_l, ms_l = [], [], [], [], []
    for l in range(DEPTH):
        lam_init = 0.8 - 0.6 * math.exp(-0.3 * l)
        lam = (jnp.exp(jnp.sum(lambda_q1[l].astype(f32) * lambda_k1[l].astype(f32)))
               - jnp.exp(jnp.sum(lambda_q2[l].astype(f32) * lambda_k2[l].astype(f32))) + lam_init)
        proj = h @ w_in[l]
        dq, dk, dv, mq, mk, mv, mo, ig, fg = jnp.split(proj, split_idx, axis=-1)
        dq = dq.reshape(-1, DA_HEADS, 2, DA_DH)
        dk = dk.reshape(-1, DA_HEADS, 2, DA_DH)
        dv = dv.reshape(-1, DA_HEADS, DA_DV)
        mq = mq.astype(f32).reshape(-1, ML_HEADS, ML_DQK) * ML_DQK ** -0.5
        mk = mk.astype(f32).reshape(-1, ML_HEADS, ML_DQK)
        mv = mv.astype(f32).reshape(-1, ML_HEADS, ML_DV)
        logi = ig.astype(f32) + b_gates[l, :ML_HEADS].astype(f32)
        logf = jax.nn.log_sigmoid(fg.astype(f32) + b_gates[l, ML_HEADS:].astype(f32))
        o_da_p = prompt_diff_attention(pad_front(prompt_part(dq)), pad_front(prompt_part(dk)), pad_front(prompt_part(dv)),
                                       rel_bias, lam)[:, META_PAD:]
        o_da_s = sample_diff_attention(sample_part(dq), sample_part(dk), sample_part(dv),
                                       cache_k[l], cache_v[l], page_table, rel_bias, lam)
        h_ml_p, (C_p, n_p, m_p) = mlstm_prompt(pad_front(prompt_part(mq)), pad_front(prompt_part(mk)), pad_front(prompt_part(mv)),
                                               pad_front(prompt_part(logi), NEG), pad_front(prompt_part(logf), 0.0))
        h_ml_p = h_ml_p[:, META_PAD:]
        (C_s, n_s, m_s), h_ml_s = mlstm_chunk(
            (state_C[l].astype(f32), state_n[l].astype(f32), state_m[l].astype(f32)),
            (sample_part(mq), sample_part(mk), sample_part(mv), sample_part(logi), sample_part(logf)))
        o_da = jnp.concatenate([o_da_p.reshape(P, DA_HEADS, DA_DV), o_da_s.reshape(DB * DS, DA_HEADS, DA_DV)], axis=0)
        o_da = (rms_norm(o_da, diff_norm_g[l]) * (1.0 - lam_init)).reshape(-1, D_DIFF)
        h_ml = jnp.concatenate([h_ml_p.reshape(P, ML_HEADS, ML_DV), h_ml_s.reshape(DB * DS, ML_HEADS, ML_DV)], axis=0)
        h_ml = jax.nn.sigmoid(mo.astype(f32)) * rms_norm(h_ml, mlstm_norm_g[l].reshape(ML_HEADS, ML_DV)).reshape(-1, D_MLSTM)
        mix = jnp.concatenate([o_da, h_ml], axis=-1).astype(dt) @ w_out[l]
        h = layer_norm(ALPHA * h + mix, ln_mix_g[l], ln_mix_b[l])
        ffn = hier_moe(h, w_router_g[l], b_router_g[l], w_router_e[l], b_router_e[l], w_gate[l], w_up[l], w_down[l])
        h = layer_norm(ALPHA * h + ffn, ln_ffn_g[l], ln_ffn_b[l])
        kp_l.append(prompt_part(dk).reshape(B, LQ, DA_HEADS, 2 * DA_DH))
        vp_l.append(prompt_part(dv))
        Cp_l.append(C_p)
        np_l.append(n_p)
        mp_l.append(m_p)
        ks_l.append(sample_part(dk).reshape(DB, DS, DA_HEADS, 2 * DA_DH))
        vs_l.append(sample_part(dv))
        Cs_l.append(C_s)
        ns_l.append(n_s)
        ms_l.append(m_s)
    y_prompt = h[:P].reshape(B, LQ, D_MODEL)[:, N_META:]
    y_sample = h[P:].reshape(DB, DS, D_MODEL)
    return (y_prompt, y_sample,
            jnp.stack(kp_l), jnp.stack(vp_l), jnp.stack(Cp_l), jnp.stack(np_l), jnp.stack(mp_l),
            jnp.stack(ks_l), jnp.stack(vs_l), jnp.stack(Cs_l), jnp.stack(ns_l), jnp.stack(ms_l))
```

```python
import functools
import math

import numpy as np
import jax
import jax.numpy as jnp
from jax import lax
from jax.experimental import pallas as pl
from jax.experimental.pallas import tpu as pltpu

F32 = jnp.float32
BF16 = jnp.bfloat16

D_MODEL = 2048
SEQ = 8192
DEC_BATCH = 32
DEC_SEQ = 8
PAGE_SIZE = 128
N_PAGES = 128
N_META = 16
BLOCK = 128
META_PAD = 112
DA_HEADS = 8
DA_DH = 64
DA_DV = 128
D_DIFF = 1024
ML_HEADS = 4
ML_DQK = 128
ML_DV = 256
D_MLSTM = 1024
D_IN = 6152
N_BUCKETS = 32
MAX_DISTANCE = 128
N_GROUPS = 4
EXPERTS_PER_GROUP = 8
N_EXPERTS = 32
D_FF = 512
ALPHA = 2.0 ** 0.25
LN_EPS = 1e-5
NEG = -1e30
LAM_INIT = 0.8 - 0.6 * math.exp(-0.0)

LANE = 128
LP = META_PAD + N_META + SEQ
N_CHUNKS = LP // BLOCK
N_SAMPLE = DEC_BATCH * DEC_SEQ
ROW_SAMPLE = LP
ROWS = LP + N_SAMPLE + BLOCK
D_IN_PAD = 6272
COL_K = 1024
COL_V = 2048
COL_MQ = 3072
COL_MK = 3584
COL_MV = 4096
COL_MO = 5120
COL_GATE = 6144

PROJ_TM = 512
PROJ_TN = 896
ATT_T = 640
ATT_NB = LP // ATT_T
PAGES_PER_STEP = 8
SA_STEPS = N_PAGES // PAGES_PER_STEP
SA_KEYS = (PAGES_PER_STEP + 1) * PAGE_SIZE
MOE_BLOCK = 128
N_ASSIGN = 2 * ROWS
N_BLOCKS = N_ASSIGN // MOE_BLOCK + N_EXPERTS
N_SLOTS = N_BLOCKS * MOE_BLOCK
ROW_TILE = 128
N_ROW_TILES = ROWS // ROW_TILE

VMEM_LIMIT = 56 * 1024 * 1024


def _cparams(sem, vmem=VMEM_LIMIT):
    return pltpu.CompilerParams(dimension_semantics=sem, vmem_limit_bytes=vmem)


def _layer_norm(x, g, b):
    mu = jnp.mean(x, axis=-1, keepdims=True)
    xc = x - mu
    var = jnp.mean(xc * xc, axis=-1, keepdims=True)
    return xc * lax.rsqrt(var + LN_EPS) * g + b


def _ln_proj_kernel(x_ref, g_ref, b_ref, w_ref, h_ref, proj_ref, xn_ref):
    @pl.when(pl.program_id(1) == 0)
    def _():
        hn = _layer_norm(x_ref[...], g_ref[...], b_ref[...])
        h_ref[...] = hn
        xn_ref[...] = hn.astype(BF16)

    proj_ref[...] = jnp.dot(xn_ref[...], w_ref[...], preferred_element_type=F32)


def _ln_proj(x, g, b, w_bf):
    return pl.pallas_call(
        _ln_proj_kernel,
        grid=(ROWS // PROJ_TM, D_IN_PAD // PROJ_TN),
        in_specs=[
            pl.BlockSpec((PROJ_TM, D_MODEL), lambda i, j: (i, 0)),
            pl.BlockSpec((1, D_MODEL), lambda i, j: (0, 0)),
            pl.BlockSpec((1, D_MODEL), lambda i, j: (0, 0)),
            pl.BlockSpec((D_MODEL, PROJ_TN), lambda i, j: (0, j)),
        ],
        out_specs=[
            pl.BlockSpec((PROJ_TM, D_MODEL), lambda i, j: (i, 0)),
            pl.BlockSpec((PROJ_TM, PROJ_TN), lambda i, j: (i, j)),
        ],
        out_shape=[
            jax.ShapeDtypeStruct((ROWS, D_MODEL), F32),
            jax.ShapeDtypeStruct((ROWS, D_IN_PAD), F32),
        ],
        scratch_shapes=[pltpu.VMEM((PROJ_TM, D_MODEL), BF16)],
        compiler_params=_cparams(("arbitrary", "arbitrary")),
        name="ln_proj",
    )(x, g, b, w_bf)


def _attn_kernel(qi_ref, kj_ref, lam_ref, q_ref, k_ref, v_ref, t_ref, o_ref,
                 bd_ref, bs_ref, m_ref, l_ref, acc_ref):
    t = pl.program_id(1)
    i = qi_ref[t]
    j = kj_ref[t]
    nsub = ATT_T // LANE

    @pl.when(t == 0)
    def _():
        t0 = t_ref[0]
        t1 = t_ref[1]
        z = jnp.zeros((LANE, LANE), F32)
        for a in range(nsub):
            bd_ref[a * LANE:(a + 1) * LANE, :] = jnp.concatenate(
                [t0 if c == a else (t1 if c == a - 1 else z) for c in range(nsub)], axis=1)
            bs_ref[a * LANE:(a + 1) * LANE, :] = jnp.concatenate(
                [t1 if (a == 0 and c == nsub - 1) else z for c in range(nsub)], axis=1)

    @pl.when(j == 0)
    def _():
        m_ref[...] = jnp.full(m_ref.shape, -jnp.inf, F32)
        l_ref[...] = jnp.zeros(l_ref.shape, F32)
        acc_ref[...] = jnp.zeros(acc_ref.shape, F32)

    lane = lax.broadcasted_iota(jnp.int32, (ATT_T, DA_DV), 1)
    q = q_ref[...] * (DA_DH ** -0.5)
    q_c = (jnp.where(lane < DA_DH, q, 0.0).astype(BF16), jnp.where(lane >= DA_DH, q, 0.0).astype(BF16))
    kb = k_ref[...].astype(BF16)
    vb = v_ref[...].astype(BF16)

    def update(bias_ref, causal, padmask):
        if causal or padmask:
            r = lax.broadcasted_iota(jnp.int32, (ATT_T, ATT_T), 0)
            c = lax.broadcasted_iota(jnp.int32, (ATT_T, ATT_T), 1)
            valid = (c + j * ATT_T) >= META_PAD
            if causal:
                valid = jnp.logical_and(valid, c <= r)
        for comp in range(2):
            s = lax.dot_general(q_c[comp], kb, (((1,), (1,)), ((), ())), preferred_element_type=F32)
            if bias_ref is not None:
                s = s + bias_ref[...]
            if causal or padmask:
                s = jnp.where(valid, s, NEG)
            m_prev = m_ref[comp]
            m_new = jnp.maximum(m_prev, jnp.max(s, axis=1, keepdims=True))
            alpha = jnp.exp(m_prev - m_new)
            p = jnp.exp(s - m_new)
            l_ref[comp] = alpha * l_ref[comp] + jnp.sum(p, axis=1, keepdims=True)
            acc_ref[comp] = alpha * acc_ref[comp] + jnp.dot(p.astype(BF16), vb, preferred_element_type=F32)
            m_ref[comp] = m_new

    @pl.when(j == i)
    def _():
        update(bd_ref, True, True)
        o_ref[...] = acc_ref[0] / l_ref[0] - lam_ref[0] * (acc_ref[1] / l_ref[1])

    @pl.when(j == i - 1)
    def _():
        update(bs_ref, False, True)

    @pl.when(jnp.logical_and(j == 0, i >= 2))
    def _():
        update(None, False, True)

    @pl.when(jnp.logical_and(j >= 1, j <= i - 2))
    def _():
        update(None, False, False)


def _prompt_attention(proj, tiles, lam):
    pairs = [(i, j) for i in range(ATT_NB) for j in range(i + 1)]
    qi = jnp.asarray([p[0] for p in pairs], jnp.int32)
    kj = jnp.asarray([p[1] for p in pairs], jnp.int32)
    grid_spec = pltpu.PrefetchScalarGridSpec(
        num_scalar_prefetch=2,
        grid=(DA_HEADS, len(pairs)),
        in_specs=[
            pl.BlockSpec(memory_space=pltpu.SMEM),
            pl.BlockSpec((ATT_T, DA_DV), lambda h, t, qi, kj: (qi[t], h)),
            pl.BlockSpec((ATT_T, DA_DV), lambda h, t, qi, kj: (kj[t], COL_K // DA_DV + h)),
            pl.BlockSpec((ATT_T, DA_DV), lambda h, t, qi, kj: (kj[t], COL_V // DA_DV + h)),
            pl.BlockSpec((None, 2, LANE, LANE), lambda h, t, qi, kj: (h, 0, 0, 0)),
        ],
        out_specs=pl.BlockSpec((ATT_T, DA_DV), lambda h, t, qi, kj: (qi[t], h)),
        scratch_shapes=[
            pltpu.VMEM((ATT_T, ATT_T), F32),
            pltpu.VMEM((ATT_T, ATT_T), F32),
            pltpu.VMEM((2, ATT_T, 1), F32),
            pltpu.VMEM((2, ATT_T, 1), F32),
            pltpu.VMEM((2, ATT_T, DA_DV), F32),
        ],
    )
    return pl.pallas_call(
        _attn_kernel,
        grid_spec=grid_spec,
        out_shape=jax.ShapeDtypeStruct((LP, D_DIFF), F32),
        compiler_params=_cparams(("arbitrary", "arbitrary")),
        name="prompt_attn",
    )(qi, kj, lam, proj, proj, proj, tiles)


def _sattn_kernel(pt_ref, lam_ref, qall_ref, knew_ref, vnew_ref, bias_ref, *refs):
    k_refs = refs[:PAGES_PER_STEP]
    v_refs = refs[PAGES_PER_STEP:2 * PAGES_PER_STEP]
    o_ref, m_ref, l_ref, acc_ref = refs[2 * PAGES_PER_STEP:]
    g = pl.program_id(1)
    ncol = DA_HEADS * 2 * DEC_SEQ

    @pl.when(g == 0)
    def _():
        m_ref[...] = jnp.full(m_ref.shape, -jnp.inf, F32)
        l_ref[...] = jnp.zeros(l_ref.shape, F32)
        acc_ref[...] = jnp.zeros(acc_ref.shape, F32)

    zpad = jnp.zeros((PAGE_SIZE - DEC_SEQ, DA_DV), F32)

    def head_rows(page_refs, new_ref, h):
        parts = [r[pl.ds(h, PAGE_SIZE, stride=DA_HEADS), :] for r in page_refs]
        parts.append(jnp.concatenate([new_ref[:, h * DA_DV:(h + 1) * DA_DV], zpad], axis=0))
        return jnp.concatenate(parts, axis=0).astype(BF16)

    kall = jnp.concatenate([head_rows(k_refs, knew_ref, h) for h in range(DA_HEADS)], axis=1)
    s = jnp.dot(kall, qall_ref[...], preferred_element_type=F32) + bias_ref[...]
    m_prev = m_ref[...]
    m_new = jnp.maximum(m_prev, jnp.max(s, axis=0, keepdims=True))
    alpha = jnp.exp(m_prev - m_new)
    p = jnp.exp(s - m_new)
    l_ref[...] = alpha * l_ref[...] + jnp.sum(p, axis=0, keepdims=True)
    m_ref[...] = m_new
    pt = p.T
    rows_per_head = 2 * DEC_SEQ
    pv = [jnp.dot(pt[h * rows_per_head:(h + 1) * rows_per_head, :].astype(BF16),
                  head_rows(v_refs, vnew_ref, h), preferred_element_type=F32)
          for h in range(DA_HEADS)]
    alpha_col = jnp.broadcast_to(alpha, (ncol, ncol)).T
    acc_ref[...] = acc_ref[...] * alpha_col + jnp.concatenate(pv, axis=0)

    @pl.when(g == SA_STEPS - 1)
    def _():
        l_col = jnp.broadcast_to(l_ref[...], (ncol, ncol)).T
        oc = acc_ref[...] / l_col
        lam = lam_ref[0]
        for h in range(DA_HEADS):
            r0 = h * rows_per_head
            o_ref[:, h * DA_DV:(h + 1) * DA_DV] = oc[r0:r0 + DEC_SEQ] - lam * oc[r0 + DEC_SEQ:r0 + 2 * DEC_SEQ]


def _sample_attention(proj, qall, bias, cache_k, cache_v, page_table, lam):
    row_blk = ROW_SAMPLE // DEC_SEQ

    def page_spec(r):
        return pl.BlockSpec((None, PAGE_SIZE * DA_HEADS, DA_DV),
                            lambda b, g, pt: (pt[b, g * PAGES_PER_STEP + r], 0, 0))

    grid_spec = pltpu.PrefetchScalarGridSpec(
        num_scalar_prefetch=1,
        grid=(DEC_BATCH, SA_STEPS),
        in_specs=[
            pl.BlockSpec(memory_space=pltpu.SMEM),
            pl.BlockSpec((None, D_DIFF, LANE), lambda b, g, pt: (b, 0, 0)),
            pl.BlockSpec((DEC_SEQ, D_DIFF), lambda b, g, pt: (row_blk + b, COL_K // D_DIFF)),
            pl.BlockSpec((DEC_SEQ, D_DIFF), lambda b, g, pt: (row_blk + b, COL_V // D_DIFF)),
            pl.BlockSpec((None, SA_KEYS, LANE), lambda b, g, pt: (g // (SA_STEPS - 1), 0, 0)),
        ] + [page_spec(r) for r in range(PAGES_PER_STEP)] * 2,
        out_specs=pl.BlockSpec((DEC_SEQ, D_DIFF), lambda b, g, pt: (b, 0)),
        scratch_shapes=[
            pltpu.VMEM((1, LANE), F32),
            pltpu.VMEM((1, LANE), F32),
            pltpu.VMEM((LANE, DA_DV), F32),
        ],
    )
    ck = cache_k.reshape(cache_k.shape[1], PAGE_SIZE * DA_HEADS, DA_DV)
    cv = cache_v.reshape(cache_v.shape[1], PAGE_SIZE * DA_HEADS, DA_DV)
    return pl.pallas_call(
        _sattn_kernel,
        grid_spec=grid_spec,
        out_shape=jax.ShapeDtypeStruct((N_SAMPLE, D_DIFF), F32),
        compiler_params=_cparams(("arbitrary", "arbitrary")),
        name="sample_attn",
    )(page_table, lam, qall, proj, proj, bias, *([ck] * PAGES_PER_STEP), *([cv] * PAGES_PER_STEP))


def _mlstm_kernel(q_ref, k_ref, v_ref, g_ref, gb_ref, c0_ref, n0_ref, m0_ref,
                  h_ref, c_ref, n_ref, m_ref, cs, ns, ms, *, rows, carry, pad_front):
    step = pl.program_id(0)
    L = BLOCK

    def load_state():
        cs[...] = c0_ref[...]
        ns[...] = n0_ref[...]
        ms[...] = m0_ref[...]

    if carry:
        pl.when(step == 0)(load_state)
    else:
        load_state()

    def pad_rows(x):
        if rows == L:
            return x
        return jnp.concatenate([x, jnp.zeros((L - rows, x.shape[1]), x.dtype)], axis=0)

    row = lax.broadcasted_iota(jnp.int32, (L, L), 0)
    col = lax.broadcasted_iota(jnp.int32, (L, L), 1)
    not_token = row >= rows
    if pad_front:
        not_token = jnp.logical_or(not_token, row + step * L < pad_front)
    pre = pad_rows(g_ref[...]) + gb_ref[...]
    logi = jnp.where(not_token, NEG, pre)
    logf = jnp.where(not_token, 0.0, jnp.minimum(pre, 0.0) - jnp.log1p(jnp.exp(-jnp.abs(pre))))
    tril = (row >= col).astype(F32)
    bcum = jnp.dot(tril, logf, preferred_element_type=F32, precision=lax.Precision.HIGHEST)
    bcum_t = bcum.T
    logi_t = logi.T

    qa = pad_rows(q_ref[...]) * (ML_DQK ** -0.5)
    ka = pad_rows(k_ref[...])
    va = pad_rows(v_ref[...])
    for h in range(ML_HEADS):
        bc = bcum[:, ML_HEADS + h:ML_HEADS + h + 1]
        br = bcum_t[ML_HEADS + h:ML_HEADS + h + 1, :]
        ic = logi[:, h:h + 1]
        ir = logi_t[h:h + 1, :]
        d = jnp.where(row >= col, (bc - br) + ir, -jnp.inf)
        m_prev = ms[h:h + 1, 0:1]
        m_inter = m_prev + bc
        m_t = jnp.maximum(m_inter, jnp.max(d, axis=1, keepdims=True))
        w = jnp.exp(d - m_t)
        qh = qa[:, h * ML_DQK:(h + 1) * ML_DQK]
        kh = ka[:, h * ML_DQK:(h + 1) * ML_DQK]
        qb = qh.astype(BF16)
        vb = va[:, h * ML_DV:(h + 1) * ML_DV].astype(BF16)
        qk = lax.dot_general(qb, kh.astype(BF16), (((1,), (1,)), ((), ())), preferred_element_type=F32)
        s = w * qk
        inter = jnp.exp(m_inter - m_t)
        c_old = cs[h]
        n_old = ns[h:h + 1, :]
        num = (jnp.dot(s.astype(BF16), vb, preferred_element_type=F32)
               + inter * jnp.dot(qb, c_old.astype(BF16), preferred_element_type=F32))
        den = jnp.sum(s, axis=1, keepdims=True) + inter * jnp.sum(qh * n_old, axis=1, keepdims=True)
        hh = num / jnp.maximum(jnp.abs(den), jnp.exp(-m_t))
        h_ref[:, h * ML_DV:(h + 1) * ML_DV] = hh[:rows]
        m_new = m_t[L - 1:L, :]
        w_end = jnp.exp(bc[L - 1:L, :] - bc + ic - m_new)
        decay = jnp.exp(m_inter[L - 1:L, :] - m_new)
        kw = kh * w_end
        cs[h] = decay * c_old + jnp.dot(kw.T.astype(BF16), vb, preferred_element_type=F32)
        ns[h:h + 1, :] = decay * n_old + jnp.sum(kw, axis=0, keepdims=True)
        ms[h:h + 1, :] = jnp.broadcast_to(m_new, (1, LANE))

    c_ref[...] = cs[...]
    n_ref[...] = ns[...]
    m_ref[...] = ms[...]


def _mlstm(proj, gate_bias, c0, n0, m0, *, n_seq, rows, carry, row_block0, n_steps, pad_front):
    seq = (lambda s: 0) if carry else (lambda s: s)
    kernel = functools.partial(_mlstm_kernel, rows=rows, carry=carry, pad_front=pad_front)
    dqk = ML_HEADS * ML_DQK
    return pl.pallas_call(
        kernel,
        grid=(n_steps,),
        in_specs=[
            pl.BlockSpec((rows, dqk), lambda s: (row_block0 + s, COL_MQ // dqk)),
            pl.BlockSpec((rows, dqk), lambda s: (row_block0 + s, COL_MK // dqk)),
            pl.BlockSpec((rows, D_MLSTM), lambda s: (row_block0 + s, COL_MV // D_MLSTM)),
            pl.BlockSpec((rows, LANE), lambda s: (row_block0 + s, COL_GATE // LANE)),
            pl.BlockSpec((1, LANE), lambda s: (0, 0)),
            pl.BlockSpec((None, ML_HEADS, ML_DQK, ML_DV), lambda s: (seq(s), 0, 0, 0)),
            pl.BlockSpec((None, ML_HEADS, ML_DQK), lambda s: (seq(s), 0, 0)),
            pl.BlockSpec((None, ML_HEADS, LANE), lambda s: (seq(s), 0, 0)),
        ],
        out_specs=[
            pl.BlockSpec((rows, D_MLSTM), lambda s: (s, 0)),
            pl.BlockSpec((None, ML_HEADS, ML_DQK, ML_DV), lambda s: (seq(s), 0, 0, 0)),
            pl.BlockSpec((None, ML_HEADS, ML_DQK), lambda s: (seq(s), 0, 0)),
            pl.BlockSpec((None, ML_HEADS, LANE), lambda s: (seq(s), 0, 0)),
        ],
        out_shape=[
            jax.ShapeDtypeStruct((n_steps * rows, D_MLSTM), F32),
            jax.ShapeDtypeStruct((n_seq, ML_HEADS, ML_DQK, ML_DV), F32),
            jax.ShapeDtypeStruct((n_seq, ML_HEADS, ML_DQK), F32),
            jax.ShapeDtypeStruct((n_seq, ML_HEADS, LANE), F32),
        ],
        scratch_shapes=[
            pltpu.VMEM((ML_HEADS, ML_DQK, ML_DV), F32),
            pltpu.VMEM((ML_HEADS, ML_DQK), F32),
            pltpu.VMEM((ML_HEADS, LANE), F32),
        ],
        compiler_params=_cparams(("arbitrary",)),
        name="mlstm_prompt" if carry else "mlstm_sample",
    )(proj, proj, proj, proj, gate_bias, c0, n0, m0)


def _mix_router_kernel(op_ref, os_ref, hp_ref, hs_ref, mo_ref, h_ref, wout_ref, dng_ref, mng_ref,
                       lng_ref, lnb_ref, wr_ref, br_ref, h2_ref, route_ref, cnt_ref, run_ref):
    i = pl.program_id(0)

    @pl.when(i == 0)
    def _():
        run_ref[...] = jnp.zeros(run_ref.shape, F32)

    is_prompt = i < N_CHUNKS
    is_sample = jnp.logical_and(i >= N_CHUNKS, i < N_CHUNKS + N_SAMPLE // ROW_TILE)
    o_da = jnp.where(is_prompt, op_ref[...], jnp.where(is_sample, os_ref[...], 0.0))
    h_ml = jnp.where(is_prompt, hp_ref[...], jnp.where(is_sample, hs_ref[...], 0.0))

    parts = []
    for hd in range(DA_HEADS):
        seg = o_da[:, hd * DA_DV:(hd + 1) * DA_DV]
        ms = jnp.mean(seg * seg, axis=1, keepdims=True)
        parts.append(seg * lax.rsqrt(ms + LN_EPS) * dng_ref[...] * (1.0 - LAM_INIT))
    mo = mo_ref[...]
    for hd in range(ML_HEADS):
        sl = slice(hd * ML_DV, (hd + 1) * ML_DV)
        seg = h_ml[:, sl]
        ms = jnp.mean(seg * seg, axis=1, keepdims=True)
        parts.append(jax.nn.sigmoid(mo[:, sl]) * (seg * lax.rsqrt(ms + LN_EPS) * mng_ref[:, sl]))
    mix_in = jnp.concatenate(parts, axis=1).astype(BF16)
    mix = jnp.dot(mix_in, wout_ref[...], preferred_element_type=F32)
    h2 = _layer_norm(ALPHA * h_ref[...] + mix, lng_ref[...], lnb_ref[...])
    h2_ref[...] = h2

    logits = jnp.dot(h2, wr_ref[...], preferred_element_type=F32, precision=lax.Precision.HIGHEST) + br_ref[...]
    lane = lax.broadcasted_iota(jnp.int32, (ROW_TILE, LANE), 1)
    ninf = -jnp.inf
    g_log = jnp.where(lane < N_GROUPS, logits, ninf)
    g_max = jnp.max(g_log, axis=1, keepdims=True)
    g_sel = jnp.min(jnp.where(g_log == g_max, lane, LANE), axis=1, keepdims=True)
    p_sel = 1.0 / jnp.sum(jnp.exp(g_log - g_max), axis=1, keepdims=True)
    lo = N_GROUPS + g_sel * EXPERTS_PER_GROUP
    in_grp = jnp.logical_and(lane >= lo, lane < lo + EXPERTS_PER_GROUP)
    e_log = jnp.where(in_grp, logits, ninf)
    v1 = jnp.max(e_log, axis=1, keepdims=True)
    i1 = jnp.min(jnp.where(e_log == v1, lane, LANE), axis=1, keepdims=True)
    e_log2 = jnp.where(lane == i1, ninf, e_log)
    v2 = jnp.max(e_log2, axis=1, keepdims=True)
    i2 = jnp.min(jnp.where(e_log2 == v2, lane, LANE), axis=1, keepdims=True)
    ex2 = jnp.exp(v2 - v1)
    gate1 = (1.0 / (1.0 + ex2)) * p_sel
    gate2 = (ex2 / (1.0 + ex2)) * p_sel
    e1 = i1 - N_GROUPS
    e2 = i2 - N_GROUPS

    oh1 = (lane == e1).astype(F32)
    oh2 = (lane == e2).astype(F32)
    both = oh1 + oh2
    r = lax.broadcasted_iota(jnp.int32, (ROW_TILE, ROW_TILE), 0)
    c = lax.broadcasted_iota(jnp.int32, (ROW_TILE, ROW_TILE), 1)
    strict = (r > c).astype(BF16)
    before = jnp.dot(strict, both.astype(BF16), preferred_element_type=F32) + run_ref[...]
    rank1 = jnp.sum(before * oh1, axis=1, keepdims=True)
    rank2 = jnp.sum(before * oh2, axis=1, keepdims=True)
    run_ref[...] = run_ref[...] + jnp.sum(both, axis=0, keepdims=True)
    cnt_ref[...] = jnp.broadcast_to(run_ref[...], cnt_ref.shape)

    out = jnp.zeros((ROW_TILE, LANE), F32)
    for idx, val in enumerate((e1.astype(F32), e2.astype(F32), gate1, gate2, rank1, rank2)):
        out = jnp.where(lane == idx, val, out)
    route_ref[...] = out


def _mix_router(o_p, o_s, hm_p, hm_s, proj, h, wout_bf, dng, mng, lng, lnb, wr, br):
    nsb = N_SAMPLE // ROW_TILE

    def prompt_map(i):
        return (jnp.minimum(i, N_CHUNKS - 1), 0)

    def sample_map(i):
        return (jnp.clip(i - N_CHUNKS, 0, nsb - 1), 0)

    const = lambda i: (0, 0)
    return pl.pallas_call(
        _mix_router_kernel,
        grid=(N_ROW_TILES,),
        in_specs=[
            pl.BlockSpec((ROW_TILE, D_DIFF), prompt_map),
            pl.BlockSpec((ROW_TILE, D_DIFF), sample_map),
            pl.BlockSpec((ROW_TILE, D_MLSTM), prompt_map),
            pl.BlockSpec((ROW_TILE, D_MLSTM), sample_map),
            pl.BlockSpec((ROW_TILE, D_MLSTM), lambda i: (i, COL_MO // D_MLSTM)),
            pl.BlockSpec((ROW_TILE, D_MODEL), lambda i: (i, 0)),
            pl.BlockSpec((D_MODEL, D_MODEL), const),
            pl.BlockSpec((1, DA_DV), const),
            pl.BlockSpec((1, D_MLSTM), const),
            pl.BlockSpec((1, D_MODEL), const),
            pl.BlockSpec((1, D_MODEL), const),
            pl.BlockSpec((D_MODEL, LANE), const),
            pl.BlockSpec((1, LANE), const),
        ],
        out_specs=[
            pl.BlockSpec((ROW_TILE, D_MODEL), lambda i: (i, 0)),
            pl.BlockSpec((ROW_TILE, LANE), lambda i: (i, 0)),
            pl.BlockSpec((8, LANE), const),
        ],
        out_shape=[
            jax.ShapeDtypeStruct((ROWS, D_MODEL), F32),
            jax.ShapeDtypeStruct((ROWS, LANE), F32),
            jax.ShapeDtypeStruct((8, LANE), F32),
        ],
        scratch_shapes=[pltpu.VMEM((1, LANE), F32)],
        compiler_params=_cparams(("arbitrary",)),
        name="mix_router",
    )(o_p, o_s, hm_p, hm_s, proj, h, wout_bf, dng, mng, lng, lnb, wr, br)


def _dispatch_kernel(dest_ref, x_ref, xb_in_ref, xb_ref, sem):
    del xb_in_ref
    base = pl.program_id(0) * (2 * ROW_TILE)

    def row_copy(r, k):
        return pltpu.make_async_copy(x_ref.at[pl.ds(r, 1)], xb_ref.at[pl.ds(dest_ref[base + 2 * r + k], 1)], sem)

    def issue(r, carry):
        row_copy(r, 0).start()
        row_copy(r, 1).start()
        return carry

    def drain(r, carry):
        row_copy(r, 0).wait()
        row_copy(r, 1).wait()
        return carry

    lax.fori_loop(0, ROW_TILE, issue, 0)
    lax.fori_loop(0, ROW_TILE, drain, 0)


def _dispatch(dest, h2):
    grid_spec = pltpu.PrefetchScalarGridSpec(
        num_scalar_prefetch=1,
        grid=(N_ROW_TILES,),
        in_specs=[
            pl.BlockSpec((ROW_TILE, D_MODEL), lambda i, d: (i, 0)),
            pl.BlockSpec(memory_space=pl.ANY),
        ],
        out_specs=pl.BlockSpec(memory_space=pl.ANY),
        scratch_shapes=[pltpu.SemaphoreType.DMA],
    )
    return pl.pallas_call(
        _dispatch_kernel,
        grid_spec=grid_spec,
        out_shape=jax.ShapeDtypeStruct((N_SLOTS, D_MODEL), F32),
        input_output_aliases={2: 0},
        compiler_params=_cparams(("arbitrary",)),
        name="moe_dispatch",
    )(dest, h2, jnp.zeros((N_SLOTS, D_MODEL), F32))


def _expert_kernel(be_ref, nu_ref, x_ref, wg_ref, wu_ref, wd_ref, y_ref, wg_s, wu_s, wd_s):
    b = pl.program_id(0)
    prev = be_ref[jnp.maximum(b - 1, 0)]

    @pl.when(jnp.logical_or(b == 0, be_ref[b] != prev))
    def _():
        wg_s[...] = wg_ref[...].astype(BF16)
        wu_s[...] = wu_ref[...].astype(BF16)
        wd_s[...] = wd_ref[...].astype(BF16)

    @pl.when(b < nu_ref[0])
    def _():
        xb = x_ref[...].astype(BF16)
        gate = jnp.dot(xb, wg_s[...], preferred_element_type=F32)
        up = jnp.dot(xb, wu_s[...], preferred_element_type=F32)
        hh = (gate * jax.nn.sigmoid(gate)) * up
        y_ref[...] = jnp.dot(hh.astype(BF16), wd_s[...], preferred_element_type=F32)

    @pl.when(b >= nu_ref[0])
    def _():
        y_ref[...] = jnp.zeros(y_ref.shape, F32)


def _experts(block_expert, n_used, xb, w_gate, w_up, w_down):
    grid_spec = pltpu.PrefetchScalarGridSpec(
        num_scalar_prefetch=2,
        grid=(N_BLOCKS,),
        in_specs=[
            pl.BlockSpec((MOE_BLOCK, D_MODEL), lambda b, be, nu: (jnp.minimum(b, nu[0] - 1), 0)),
            pl.BlockSpec((None, None, D_MODEL, D_FF), lambda b, be, nu: (0, be[b], 0, 0)),
            pl.BlockSpec((None, None, D_MODEL, D_FF), lambda b, be, nu: (0, be[b], 0, 0)),
            pl.BlockSpec((None, None, D_FF, D_MODEL), lambda b, be, nu: (0, be[b], 0, 0)),
        ],
        out_specs=pl.BlockSpec((MOE_BLOCK, D_MODEL), lambda b, be, nu: (b, 0)),
        scratch_shapes=[
            pltpu.VMEM((D_MODEL, D_FF), BF16),
            pltpu.VMEM((D_MODEL, D_FF), BF16),
            pltpu.VMEM((D_FF, D_MODEL), BF16),
        ],
    )
    return pl.pallas_call(
        _expert_kernel,
        grid_spec=grid_spec,
        out_shape=jax.ShapeDtypeStruct((N_SLOTS, D_MODEL), F32),
        compiler_params=_cparams(("arbitrary",)),
        name="moe_experts",
    )(block_expert, n_used, xb, w_gate, w_up, w_down)


def _combine_kernel(dest_ref, h2_ref, route_ref, g_ref, b_ref, yb_ref, out_ref, buf, sem):
    base = pl.program_id(0) * (2 * ROW_TILE)

    def row_copy(r, k):
        return pltpu.make_async_copy(yb_ref.at[pl.ds(dest_ref[base + 2 * r + k], 1)], buf.at[k, pl.ds(r, 1)], sem)

    def issue(r, carry):
        row_copy(r, 0).start()
        row_copy(r, 1).start()
        return carry

    def drain(r, carry):
        row_copy(r, 0).wait()
        row_copy(r, 1).wait()
        return carry

    lax.fori_loop(0, ROW_TILE, issue, 0)
    lax.fori_loop(0, ROW_TILE, drain, 0)
    route = route_ref[...]
    y = route[:, 2:3] * buf[0] + route[:, 3:4] * buf[1]
    out_ref[...] = _layer_norm(ALPHA * h2_ref[...] + y, g_ref[...], b_ref[...])


def _combine(dest, h2, route, g, b, yb):
    grid_spec = pltpu.PrefetchScalarGridSpec(
        num_scalar_prefetch=1,
        grid=(N_ROW_TILES,),
        in_specs=[
            pl.BlockSpec((ROW_TILE, D_MODEL), lambda i, d: (i, 0)),
            pl.BlockSpec((ROW_TILE, LANE), lambda i, d: (i, 0)),
            pl.BlockSpec((1, D_MODEL), lambda i, d: (0, 0)),
            pl.BlockSpec((1, D_MODEL), lambda i, d: (0, 0)),
            pl.BlockSpec(memory_space=pl.ANY),
        ],
        out_specs=pl.BlockSpec((ROW_TILE, D_MODEL), lambda i, d: (i, 0)),
        scratch_shapes=[pltpu.VMEM((2, ROW_TILE, D_MODEL), F32), pltpu.SemaphoreType.DMA],
    )
    return pl.pallas_call(
        _combine_kernel,
        grid_spec=grid_spec,
        out_shape=jax.ShapeDtypeStruct((ROWS, D_MODEL), F32),
        compiler_params=_cparams(("arbitrary",)),
        name="moe_combine",
    )(dest, h2, route, g, b, yb)


def _t5_bucket(rel):
    n = jnp.maximum(rel, 0)
    max_exact = N_BUCKETS // 2
    nf = jnp.maximum(n, max_exact).astype(F32)
    large = max_exact + (jnp.log(nf / max_exact) / math.log(MAX_DISTANCE / max_exact)
                         * (N_BUCKETS - max_exact)).astype(jnp.int32)
    large = jnp.minimum(large, N_BUCKETS - 1)
    return jnp.where(n < max_exact, n, large)


def _rel_bias(rel_bias, rel):
    table = rel_bias.astype(F32) - rel_bias.astype(F32)[N_BUCKETS - 1][None, :]
    return jnp.moveaxis(table[_t5_bucket(rel)], -1, 0)


def kernel(x_prompt, x_sample, cache_k, cache_v, state_C, state_n, state_m, page_table, meta_tokens, ln_in_g, ln_in_b, rel_bias, w_in, b_gates, lambda_q1, lambda_k1, lambda_q2, lambda_k2, diff_norm_g, mlstm_norm_g, w_out, ln_mix_g, ln_mix_b, w_router_g, b_router_g, w_router_e, b_router_e, w_gate, w_up, w_down, ln_ffn_g, ln_ffn_b):
    lam = (jnp.exp(jnp.sum(lambda_q1[0].astype(F32) * lambda_k1[0].astype(F32)))
           - jnp.exp(jnp.sum(lambda_q2[0].astype(F32) * lambda_k2[0].astype(F32))) + LAM_INIT).reshape(1)

    x = jnp.concatenate([
        jnp.zeros((META_PAD, D_MODEL), F32), meta_tokens.astype(F32), x_prompt.reshape(SEQ, D_MODEL),
        x_sample.reshape(N_SAMPLE, D_MODEL), jnp.zeros((ROWS - LP - N_SAMPLE, D_MODEL), F32)], axis=0)
    w_in_bf = jnp.pad(w_in[0].astype(BF16), ((0, 0), (0, D_IN_PAD - D_IN)))
    h, proj = _ln_proj(x, ln_in_g.reshape(1, D_MODEL), ln_in_b.reshape(1, D_MODEL), w_in_bf)

    ar = jnp.arange(LANE, dtype=jnp.int32)
    rel0 = ar[:, None] - ar[None, :]
    tiles = jnp.stack([_rel_bias(rel_bias, rel0), _rel_bias(rel_bias, rel0 + LANE)], axis=1)
    o_p = _prompt_attention(proj, tiles, lam)

    q_s = proj[ROW_SAMPLE:ROW_SAMPLE + N_SAMPLE, :D_DIFF].reshape(DEC_BATCH, DEC_SEQ, DA_HEADS, 2, DA_DH)
    q6 = jnp.transpose(q_s * (DA_DH ** -0.5), (0, 2, 3, 4, 1))
    eye_h = jnp.eye(DA_HEADS, dtype=F32)
    eye_c = jnp.eye(2, dtype=F32)
    qall = (q6[:, :, :, :, None, None, :] * eye_h[None, :, None, None, :, None, None]
            * eye_c[None, None, :, None, None, :, None]).reshape(DEC_BATCH, D_DIFF, LANE).astype(BF16)
    qpos = jnp.arange(DEC_SEQ, dtype=jnp.int32)
    tpos = jnp.arange(PAGE_SIZE, dtype=jnp.int32)
    b_last = _rel_bias(rel_bias, PAGE_SIZE + qpos[None, :] - tpos[:, None])
    rel_new = qpos[None, :] - qpos[:, None]
    b_new = jnp.where((rel_new >= 0)[None], _rel_bias(rel_bias, rel_new), NEG)

    def cols(a):
        return jnp.broadcast_to(jnp.transpose(a, (1, 0, 2))[:, :, None, :],
                                (a.shape[1], DA_HEADS, 2, DEC_SEQ)).reshape(a.shape[1], LANE)

    past = PAGES_PER_STEP * PAGE_SIZE
    bias0 = jnp.concatenate([jnp.zeros((past, LANE), F32), jnp.full((PAGE_SIZE, LANE), NEG, F32)], axis=0)
    bias1 = jnp.concatenate([jnp.zeros((past - PAGE_SIZE, LANE), F32), cols(b_last), cols(b_new),
                             jnp.full((PAGE_SIZE - DEC_SEQ, LANE), NEG, F32)], axis=0)
    o_s = _sample_attention(proj, qall, jnp.stack([bias0, bias1]), cache_k, cache_v,
                            page_table.astype(jnp.int32), lam)

    gate_bias = jnp.pad(b_gates[0].astype(F32), (0, LANE - 2 * ML_HEADS)).reshape(1, LANE)
    hm_p, c_p, n_p, m_p = _mlstm(
        proj, gate_bias, jnp.zeros((1, ML_HEADS, ML_DQK, ML_DV), F32), jnp.zeros((1, ML_HEADS, ML_DQK), F32),
        jnp.zeros((1, ML_HEADS, LANE), F32), n_seq=1, rows=BLOCK, carry=True, row_block0=0,
        n_steps=N_CHUNKS, pad_front=META_PAD)
    hm_s, c_s, n_s, m_s = _mlstm(
        proj, gate_bias, state_C[0].astype(F32), state_n[0].astype(F32),
        jnp.broadcast_to(state_m[0].astype(F32)[:, :, None], (DEC_BATCH, ML_HEADS, LANE)),
        n_seq=DEC_BATCH, rows=DEC_SEQ, carry=False, row_block0=ROW_SAMPLE // DEC_SEQ,
        n_steps=DEC_BATCH, pad_front=0)

    wr = jnp.pad(jnp.concatenate([w_router_g[0], w_router_e[0]], axis=1).astype(F32),
                 ((0, 0), (0, LANE - N_GROUPS - N_EXPERTS)))
    br = jnp.pad(jnp.concatenate([b_router_g[0], b_router_e[0]]).astype(F32),
                 (0, LANE - N_GROUPS - N_EXPERTS)).reshape(1, LANE)
    h2, route, counts = _mix_router(
        o_p, o_s, hm_p, hm_s, proj, h, w_out[0].astype(BF16), diff_norm_g[0].reshape(1, DA_DV),
        mlstm_norm_g[0].reshape(1, D_MLSTM), ln_mix_g[0].reshape(1, D_MODEL), ln_mix_b[0].reshape(1, D_MODEL),
        wr, br)

    cnt = counts[0, :N_EXPERTS].astype(jnp.int32)
    padded = (cnt + MOE_BLOCK - 1) // MOE_BLOCK * MOE_BLOCK
    pend = jnp.cumsum(padded)
    pstart = pend - padded
    e12 = route[:, 0:2].astype(jnp.int32)
    dest = (pstart[e12] + route[:, 4:6].astype(jnp.int32)).reshape(N_ASSIGN)
    block_start = jnp.arange(N_BLOCKS, dtype=jnp.int32) * MOE_BLOCK
    block_expert = jnp.minimum(jnp.sum(block_start[:, None] >= pend[None, :], axis=1), N_EXPERTS - 1).astype(jnp.int32)
    n_used = (pend[-1] // MOE_BLOCK).astype(jnp.int32).reshape(1)

    xb = _dispatch(dest, h2)
    yb = _experts(block_expert, n_used, xb, w_gate, w_up, w_down)
    h3 = _combine(dest, h2, route, ln_ffn_g[0].reshape(1, D_MODEL), ln_ffn_b[0].reshape(1, D_MODEL), yb)

    lq = N_META + SEQ
    y_prompt = h3[BLOCK:LP].reshape(1, SEQ, D_MODEL)
    y_sample = h3[ROW_SAMPLE:ROW_SAMPLE + N_SAMPLE].reshape(DEC_BATCH, DEC_SEQ, D_MODEL)
    k_prompt = proj[META_PAD:LP, COL_K:COL_K + D_DIFF].reshape(1, 1, lq, DA_HEADS, 2 * DA_DH)
    v_prompt = proj[META_PAD:LP, COL_V:COL_V + D_DIFF].reshape(1, 1, lq, DA_HEADS, DA_DV)
    k_sample = proj[ROW_SAMPLE:ROW_SAMPLE + N_SAMPLE, COL_K:COL_K + D_DIFF].reshape(1, DEC_BATCH, DEC_SEQ, DA_HEADS, 2 * DA_DH)
    v_sample = proj[ROW_SAMPLE:ROW_SAMPLE + N_SAMPLE, COL_V:COL_V + D_DIFF].reshape(1, DEC_BATCH, DEC_SEQ, DA_HEADS, DA_DV)
    return (y_prompt, y_sample, k_prompt, v_prompt,
            c_p[None], n_p[None], m_p[None, :, :, 0],
            k_sample, v_sample, c_s[None], n_s[None], m_s[None, :, :, 0])
```

```python
import functools
import math

import numpy as np
import jax
import jax.numpy as jnp
from jax import lax
from jax.experimental import pallas as pl
from jax.experimental.pallas import tpu as pltpu

F32 = jnp.float32
BF16 = jnp.bfloat16

D_MODEL = 2048
SEQ = 8192
DEC_BATCH = 32
DEC_SEQ = 8
PAGE_SIZE = 128
N_PAGES = 128
N_META = 16
BLOCK = 128
META_PAD = 112
DA_HEADS = 8
DA_DH = 64
DA_DV = 128
D_DIFF = 1024
ML_HEADS = 4
ML_DQK = 128
ML_DV = 256
D_MLSTM = 1024
D_IN = 6152
N_BUCKETS = 32
MAX_DISTANCE = 128
N_GROUPS = 4
EXPERTS_PER_GROUP = 8
N_EXPERTS = 32
D_FF = 512
ALPHA = 2.0 ** 0.25
LN_EPS = 1e-5
NEG = -1e30
LAM_INIT = 0.8 - 0.6 * math.exp(-0.0)
LOG2E = math.log2(math.e)
Q_SCALE = DA_DH ** -0.5 * LOG2E

LANE = 128
LP = META_PAD + N_META + SEQ
N_CHUNKS = LP // BLOCK
N_SAMPLE = DEC_BATCH * DEC_SEQ
ROW_SAMPLE = LP
ROWS = LP + N_SAMPLE + BLOCK
D_MAIN = 6144
COL_K = 1024
COL_V = 2048
COL_MQ = 3072
COL_MK = 3584
COL_MV = 4096
COL_MO = 5120

PROJ_TM = 512
PROJ_TN = 1024
ATT_T = 640
ATT_NB = LP // ATT_T
ATT_HEADS = 2
ATT_ONES = 16
PAGES_PER_STEP = 8
PAGE_LOOKAHEAD = 2
PAGE_SLOTS = PAGE_LOOKAHEAD + 1
SA_STEPS = N_PAGES // PAGES_PER_STEP
PAGE_ROWS = PAGE_SIZE * DA_HEADS
SA_COLS = DA_HEADS * 2 * DEC_SEQ
MOE_BLOCK = 256
N_ASSIGN = 2 * ROWS
N_BLOCKS = N_ASSIGN // MOE_BLOCK + N_EXPERTS
N_SLOTS = N_BLOCKS * MOE_BLOCK
ROW_TILE = 128
N_ROW_TILES = ROWS // ROW_TILE
MIX_TM = 512
DMA_UNROLL = 8

VMEM_LIMIT = 56 * 1024 * 1024


def _cparams(sem, vmem=VMEM_LIMIT):
    return pltpu.CompilerParams(dimension_semantics=sem, vmem_limit_bytes=vmem)


def _layer_norm(x, g, b):
    mu = jnp.mean(x, axis=-1, keepdims=True)
    xc = x - mu
    var = jnp.mean(xc * xc, axis=-1, keepdims=True)
    return xc * lax.rsqrt(var + LN_EPS) * g + b


_NT = (((1,), (1,)), ((), ()))


def _ln_proj_kernel(x_ref, g_ref, b_ref, w_ref, wg_ref, h_ref, proj_ref, gate_ref,
                    qlo_ref, qhi_ref, kb_ref, vb_ref, xn_ref):
    j = pl.program_id(1)

    @pl.when(j == 0)
    def _():
        hn = _layer_norm(x_ref[...], g_ref[...], b_ref[...])
        h_ref[...] = hn
        xn = hn.astype(BF16)
        xn_ref[...] = xn
        gate_ref[...] = jnp.dot(xn, wg_ref[...], preferred_element_type=F32)

    acc = jnp.dot(xn_ref[...], w_ref[...], preferred_element_type=F32)
    proj_ref[...] = acc

    @pl.when(j == 0)
    def _():
        lane = lax.broadcasted_iota(jnp.int32, acc.shape, 1)
        first = (lane & (DA_DV - 1)) < DA_DH
        qs = acc * Q_SCALE
        qlo_ref[...] = jnp.where(first, qs, 0.0).astype(BF16)
        qhi_ref[...] = jnp.where(first, 0.0, qs).astype(BF16)

    @pl.when(j == 1)
    def _():
        kb_ref[...] = acc.astype(BF16)

    @pl.when(j == 2)
    def _():
        vb_ref[...] = acc.T.astype(BF16)


def _ln_proj(x, g, b, w_bf, wg_bf):
    row = lambda i, j: (i, 0)
    bf_shape = jax.ShapeDtypeStruct((ROWS, D_DIFF), BF16)
    return pl.pallas_call(
        _ln_proj_kernel,
        grid=(ROWS // PROJ_TM, D_MAIN // PROJ_TN),
        in_specs=[
            pl.BlockSpec((PROJ_TM, D_MODEL), row),
            pl.BlockSpec((1, D_MODEL), lambda i, j: (0, 0)),
            pl.BlockSpec((1, D_MODEL), lambda i, j: (0, 0)),
            pl.BlockSpec((D_MODEL, PROJ_TN), lambda i, j: (0, j)),
            pl.BlockSpec((D_MODEL, LANE), lambda i, j: (0, 0)),
        ],
        out_specs=[
            pl.BlockSpec((PROJ_TM, D_MODEL), row),
            pl.BlockSpec((PROJ_TM, PROJ_TN), lambda i, j: (i, j)),
            pl.BlockSpec((PROJ_TM, LANE), row),
            pl.BlockSpec((PROJ_TM, D_DIFF), row),
            pl.BlockSpec((PROJ_TM, D_DIFF), row),
            pl.BlockSpec((PROJ_TM, D_DIFF), row),
            pl.BlockSpec((D_DIFF, PROJ_TM), lambda i, j: (0, i)),
        ],
        out_shape=[
            jax.ShapeDtypeStruct((ROWS, D_MODEL), F32),
            jax.ShapeDtypeStruct((ROWS, D_MAIN), F32),
            jax.ShapeDtypeStruct((ROWS, LANE), F32),
            bf_shape, bf_shape, bf_shape,
            jax.ShapeDtypeStruct((D_DIFF, ROWS), BF16),
        ],
        scratch_shapes=[pltpu.VMEM((PROJ_TM, D_MODEL), BF16)],
        compiler_params=_cparams(("arbitrary", "arbitrary")),
        name="ln_proj",
    )(x, g, b, w_bf, wg_bf)


def _attn_kernel(qi_ref, kj_ref, lam_ref, qlo_ref, qhi_ref, k_ref, vt_ref, t_ref, o_init_ref, o_ref,
                 bd_ref, bs_ref, *state_refs):
    m_refs = state_refs[:2 * ATT_HEADS]
    acc_refs = state_refs[2 * ATT_HEADS:]
    del o_init_ref
    t = pl.program_id(1)
    i = qi_ref[t]
    j = kj_ref[t]
    nsub = ATT_T // LANE

    @pl.when(t == 0)
    def _():
        z = jnp.zeros((LANE, LANE), F32)
        for a in range(ATT_HEADS):
            t0 = t_ref[a, 0]
            t1 = t_ref[a, 1]
            for r in range(nsub):
                bd_ref[a, r * LANE:(r + 1) * LANE, :] = jnp.concatenate(
                    [t0 if c == r else (t1 if c == r + 1 else z) for c in range(nsub)], axis=1)
                bs_ref[a, r * LANE:(r + 1) * LANE, :] = jnp.concatenate(
                    [t1 if (r == nsub - 1 and c == 0) else z for c in range(nsub)], axis=1)

    @pl.when(j == 0)
    def _():
        for m_ref, acc_ref in zip(m_refs, acc_refs):
            m_ref[...] = jnp.full(m_ref.shape, -jnp.inf, F32)
            acc_ref[...] = jnp.zeros(acc_ref.shape, F32)

    ones_rows = jnp.ones((ATT_ONES, ATT_T), BF16)

    def update(bias_ref, causal, padmask):
        if causal or padmask:
            r = lax.broadcasted_iota(jnp.int32, (ATT_T, ATT_T), 0)
            c = lax.broadcasted_iota(jnp.int32, (ATT_T, ATT_T), 1)
            valid = (r + j * ATT_T) >= META_PAD
            if causal:
                valid = jnp.logical_and(valid, r <= c)
        def score(idx):
            a, comp = divmod(idx, 2)
            sl = slice(a * DA_DV, (a + 1) * DA_DV)
            q_ref = (qlo_ref, qhi_ref)[comp]
            s = lax.dot_general(k_ref[:, sl], q_ref[:, sl], _NT, preferred_element_type=F32)
            if bias_ref is not None:
                s = s + bias_ref[a]
            if causal or padmask:
                s = jnp.where(valid, s, NEG)
            return s

        def accumulate(idx, s, m_prev, m_new):
            a = idx // 2
            vt = jnp.concatenate([vt_ref[a * DA_DV:(a + 1) * DA_DV, :], ones_rows], axis=0)
            acc_ref = acc_refs[idx]
            alpha = jnp.exp2(m_prev - m_new)
            p = jnp.exp2(s - m_new).astype(BF16)
            acc_ref[...] = alpha * acc_ref[...] + jnp.dot(vt, p, preferred_element_type=F32)

        n_chain = 2 * ATT_HEADS
        scores = [score(idx) for idx in range(n_chain)]
        m_prevs = [m_refs[idx][...] for idx in range(n_chain)]
        m_news = [jnp.maximum(mp, jnp.max(s, axis=0, keepdims=True)) for mp, s in zip(m_prevs, scores)]
        for idx in range(n_chain):
            m_refs[idx][...] = m_news[idx]
        for idx in range(n_chain):
            accumulate(idx, scores[idx], m_prevs[idx], m_news[idx])

    @pl.when(j == i)
    def _():
        update(bd_ref, True, True)
        for a in range(ATT_HEADS):
            n1 = acc_refs[2 * a][...]
            n2 = acc_refs[2 * a + 1][...]
            ot = (n1[:DA_DV] / n1[DA_DV:DA_DV + 1] - lam_ref[0] * (n2[:DA_DV] / n2[DA_DV:DA_DV + 1]))
            o_ref[:, a * DA_DV:(a + 1) * DA_DV] = ot.T

    @pl.when(j == i - 1)
    def _():
        update(bs_ref, False, True)

    @pl.when(jnp.logical_and(j == 0, i >= 2))
    def _():
        update(None, False, True)

    @pl.when(jnp.logical_and(j >= 1, j <= i - 2))
    def _():
        update(None, False, False)


def _prompt_attention(qlo, qhi, kb, vt, tiles, lam, o_init):
    pairs = [(i, j) for i in range(ATT_NB) for j in range(i + 1)]
    qi = jnp.asarray([p[0] for p in pairs], jnp.int32)
    kj = jnp.asarray([p[1] for p in pairs], jnp.int32)
    width = ATT_HEADS * DA_DV
    qmap = lambda h, t, qi, kj: (qi[t], h)
    kmap = lambda h, t, qi, kj: (kj[t], h)
    grid_spec = pltpu.PrefetchScalarGridSpec(
        num_scalar_prefetch=2,
        grid=(DA_HEADS // ATT_HEADS, len(pairs)),
        in_specs=[
            pl.BlockSpec(memory_space=pltpu.SMEM),
            pl.BlockSpec((ATT_T, width), qmap),
            pl.BlockSpec((ATT_T, width), qmap),
            pl.BlockSpec((ATT_T, width), kmap),
            pl.BlockSpec((width, ATT_T), lambda h, t, qi, kj: (h, kj[t])),
            pl.BlockSpec((ATT_HEADS, 2, LANE, LANE), lambda h, t, qi, kj: (h, 0, 0, 0)),
            pl.BlockSpec(memory_space=pl.ANY),
        ],
        out_specs=pl.BlockSpec((ATT_T, width), qmap),
        scratch_shapes=[
            pltpu.VMEM((ATT_HEADS, ATT_T, ATT_T), F32),
            pltpu.VMEM((ATT_HEADS, ATT_T, ATT_T), F32),
        ] + [pltpu.VMEM((1, ATT_T), F32)] * (2 * ATT_HEADS) + [
            pltpu.VMEM((DA_DV + ATT_ONES, ATT_T), F32)] * (2 * ATT_HEADS) + [
        ],
    )
    return pl.pallas_call(
        _attn_kernel,
        grid_spec=grid_spec,
        out_shape=jax.ShapeDtypeStruct((ROWS, D_DIFF), F32),
        input_output_aliases={8: 0},
        compiler_params=_cparams(("arbitrary", "arbitrary")),
        name="prompt_attn",
    )(qi, kj, lam, qlo, qhi, kb, vt, tiles, o_init)


def _sattn_kernel(pt_ref, lam_ref, qt_ref, knew_ref, vnew_ref, bm_ref, bmnew_ref, ck_ref, cv_ref, o_init_ref,
                  o_ref, m_ref, l_ref, acc_ref, kbuf, vbuf, sem):
    del o_init_ref
    b = pl.program_id(0)
    g = pl.program_id(1)
    step = b * SA_STEPS + g
    n_steps = DEC_BATCH * SA_STEPS
    last = g == SA_STEPS - 1

    def page_copies(step_, slot_):
        seq = step_ // SA_STEPS
        first = (step_ - seq * SA_STEPS) * PAGES_PER_STEP
        copies = []
        for r in range(PAGES_PER_STEP):
            page = pt_ref[seq, first + r]
            copies.append(pltpu.make_async_copy(ck_ref.at[page], kbuf.at[slot_, r], sem.at[slot_]))
            copies.append(pltpu.make_async_copy(cv_ref.at[page], vbuf.at[slot_, r], sem.at[slot_]))
        return copies

    def fetch(step_):
        for c in page_copies(step_, lax.rem(step_, PAGE_SLOTS)):
            c.start()

    @pl.when(step == 0)
    def _():
        for ahead in range(PAGE_LOOKAHEAD):
            fetch(step + ahead)

    @pl.when(step + PAGE_LOOKAHEAD < n_steps)
    def _():
        fetch(step + PAGE_LOOKAHEAD)

    slot = lax.rem(step, PAGE_SLOTS)
    for c in page_copies(step, slot):
        c.wait()

    @pl.when(g == 0)
    def _():
        m_ref[...] = jnp.full(m_ref.shape, -jnp.inf, F32)
        l_ref[...] = jnp.zeros(l_ref.shape, F32)
        acc_ref[...] = jnp.zeros(acc_ref.shape, F32)

    qt = qt_ref[...]
    is_last = last.astype(jnp.int32)

    k_blocks = [kbuf[slot, r] for r in range(PAGES_PER_STEP)] + [knew_ref[...]]
    v_blocks = [vbuf[slot, r] for r in range(PAGES_PER_STEP)] + [vnew_ref[...]]
    biases = [bm_ref[0]] * (PAGES_PER_STEP - 1) + [bm_ref[is_last], bmnew_ref[is_last]]
    scores = [lax.dot_general(qt, kr.astype(BF16), _NT, preferred_element_type=F32) + b
              for kr, b in zip(k_blocks, biases)]
    m_prev = m_ref[...]
    m_new = m_prev
    for s in scores:
        m_new = jnp.maximum(m_new, jnp.max(s, axis=1, keepdims=True))
    alpha = jnp.exp2(m_prev - m_new)
    l_new = alpha * l_ref[...]
    acc_new = alpha * acc_ref[...]
    for s, vr in zip(scores, v_blocks):
        p = jnp.exp2(s - m_new)
        l_new = l_new + jnp.sum(p, axis=1, keepdims=True)
        acc_new = acc_new + jnp.dot(p.astype(BF16), vr.astype(BF16), preferred_element_type=F32)
    m_ref[...] = m_new
    l_ref[...] = l_new
    acc_ref[...] = acc_new

    @pl.when(last)
    def _():
        oc = acc_new / l_new
        lam = lam_ref[0]
        rows_per_head = 2 * DEC_SEQ
        for h in range(DA_HEADS):
            r0 = h * rows_per_head
            o_ref[:, h * DA_DV:(h + 1) * DA_DV] = oc[r0:r0 + DEC_SEQ] - lam * oc[r0 + DEC_SEQ:r0 + 2 * DEC_SEQ]


def _sample_attention(qt, knew, vnew, bm, bmnew, cache_k, cache_v, page_table, lam, o_init):
    per_seq = lambda b, g, pt: (b, 0, 0)
    any_spec = pl.BlockSpec(memory_space=pl.ANY)
    grid_spec = pltpu.PrefetchScalarGridSpec(
        num_scalar_prefetch=1,
        grid=(DEC_BATCH, SA_STEPS),
        in_specs=[
            pl.BlockSpec(memory_space=pltpu.SMEM),
            pl.BlockSpec((None, SA_COLS, DA_DV), per_seq),
            pl.BlockSpec((None, LANE, DA_DV), per_seq),
            pl.BlockSpec((None, LANE, DA_DV), per_seq),
            pl.BlockSpec((2, SA_COLS, PAGE_ROWS), lambda b, g, pt: (0, 0, 0)),
            pl.BlockSpec((2, SA_COLS, LANE), lambda b, g, pt: (0, 0, 0)),
            any_spec, any_spec, any_spec,
        ],
        out_specs=pl.BlockSpec((DEC_SEQ, D_DIFF), lambda b, g, pt: (ROW_SAMPLE // DEC_SEQ + b, 0)),
        scratch_shapes=[
            pltpu.VMEM((SA_COLS, 1), F32),
            pltpu.VMEM((SA_COLS, 1), F32),
            pltpu.VMEM((SA_COLS, DA_DV), F32),
            pltpu.VMEM((PAGE_SLOTS, PAGES_PER_STEP, PAGE_ROWS, DA_DV), F32),
            pltpu.VMEM((PAGE_SLOTS, PAGES_PER_STEP, PAGE_ROWS, DA_DV), F32),
            pltpu.SemaphoreType.DMA((PAGE_SLOTS,)),
        ],
    )
    ck = cache_k.reshape(cache_k.shape[1], PAGE_ROWS, DA_DV)
    cv = cache_v.reshape(cache_v.shape[1], PAGE_ROWS, DA_DV)
    return pl.pallas_call(
        _sattn_kernel,
        grid_spec=grid_spec,
        out_shape=jax.ShapeDtypeStruct((ROWS, D_DIFF), F32),
        input_output_aliases={9: 0},
        compiler_params=_cparams(("arbitrary", "arbitrary")),
        name="sample_attn",
    )(page_table, lam, qt, knew, vnew, bm, bmnew, ck, cv, o_init)


def _mlstm_kernel(q_ref, k_ref, v_ref, g_ref, gb_ref, c0_ref, n0_ref, m0_ref, h_init_ref,
                  h_ref, c_ref, n_ref, m_ref, cs, ns, ms, *, rows, carry, pad_front):
    del h_init_ref
    step = pl.program_id(0)
    L = BLOCK

    def load_state():
        cs[...] = c0_ref[...]
        ns[...] = n0_ref[...]
        ms[...] = m0_ref[...]

    if carry:
        pl.when(step == 0)(load_state)
    else:
        load_state()

    def pad_rows(x):
        if rows == L:
            return x
        return jnp.concatenate([x, jnp.zeros((L - rows, x.shape[1]), x.dtype)], axis=0)

    row = lax.broadcasted_iota(jnp.int32, (L, L), 0)
    col = lax.broadcasted_iota(jnp.int32, (L, L), 1)
    not_token = row >= rows
    if pad_front:
        not_token = jnp.logical_or(not_token, row + step * L < pad_front)
    pre = pad_rows(g_ref[...]) + gb_ref[...]
    logi = jnp.where(not_token, NEG, pre)
    logf = jnp.where(not_token, 0.0, jnp.minimum(pre, 0.0) - jnp.log1p(jnp.exp(-jnp.abs(pre))))
    tril = (row >= col).astype(F32)
    bcum = jnp.dot(tril, logf, preferred_element_type=F32, precision=lax.Precision.HIGHEST)
    bcum_t = bcum.T
    logi_t = logi.T

    qa = pad_rows(q_ref[...]) * (ML_DQK ** -0.5)
    ka = pad_rows(k_ref[...])
    va = pad_rows(v_ref[...])
    for h in range(ML_HEADS):
        bc = bcum[:, ML_HEADS + h:ML_HEADS + h + 1]
        br = bcum_t[ML_HEADS + h:ML_HEADS + h + 1, :]
        ic = logi[:, h:h + 1]
        ir = logi_t[h:h + 1, :]
        d = jnp.where(row >= col, (bc - br) + ir, -jnp.inf)
        m_prev = ms[h:h + 1, 0:1]
        m_inter = m_prev + bc
        m_t = jnp.maximum(m_inter, jnp.max(d, axis=1, keepdims=True))
        w = jnp.exp(d - m_t)
        qh = qa[:, h * ML_DQK:(h + 1) * ML_DQK]
        kh = ka[:, h * ML_DQK:(h + 1) * ML_DQK]
        qb = qh.astype(BF16)
        vb = va[:, h * ML_DV:(h + 1) * ML_DV].astype(BF16)
        qk = lax.dot_general(qb, kh.astype(BF16), _NT, preferred_element_type=F32)
        s = w * qk
        inter = jnp.exp(m_inter - m_t)
        c_old = cs[h]
        n_old = ns[h:h + 1, :]
        num = (jnp.dot(s.astype(BF16), vb, preferred_element_type=F32)
               + inter * jnp.dot(qb, c_old.astype(BF16), preferred_element_type=F32))
        den = jnp.sum(s, axis=1, keepdims=True) + inter * jnp.sum(qh * n_old, axis=1, keepdims=True)
        hh = num / jnp.maximum(jnp.abs(den), jnp.exp(-m_t))
        h_ref[:, h * ML_DV:(h + 1) * ML_DV] = hh[:rows]
        m_new = m_t[L - 1:L, :]
        w_end = jnp.exp(bc[L - 1:L, :] - bc + ic - m_new)
        decay = jnp.exp(m_inter[L - 1:L, :] - m_new)
        kw = kh * w_end
        cs[h] = decay * c_old + jnp.dot(kw.T.astype(BF16), vb, preferred_element_type=F32)
        ns[h:h + 1, :] = decay * n_old + jnp.sum(kw, axis=0, keepdims=True)
        ms[h:h + 1, :] = jnp.broadcast_to(m_new, (1, LANE))

    c_ref[...] = cs[...]
    n_ref[...] = ns[...]
    m_ref[...] = ms[...]


def _mlstm(proj, gates, gate_bias, c0, n0, m0, h_init, *, n_seq, rows, carry, row_block0, n_steps, pad_front):
    seq = (lambda s: 0) if carry else (lambda s: s)
    kernel = functools.partial(_mlstm_kernel, rows=rows, carry=carry, pad_front=pad_front)
    dqk = ML_HEADS * ML_DQK
    return pl.pallas_call(
        kernel,
        grid=(n_steps,),
        in_specs=[
            pl.BlockSpec((rows, dqk), lambda s: (row_block0 + s, COL_MQ // dqk)),
            pl.BlockSpec((rows, dqk), lambda s: (row_block0 + s, COL_MK // dqk)),
            pl.BlockSpec((rows, D_MLSTM), lambda s: (row_block0 + s, COL_MV // D_MLSTM)),
            pl.BlockSpec((rows, LANE), lambda s: (row_block0 + s, 0)),
            pl.BlockSpec((1, LANE), lambda s: (0, 0)),
            pl.BlockSpec((None, ML_HEADS, ML_DQK, ML_DV), lambda s: (seq(s), 0, 0, 0)),
            pl.BlockSpec((None, ML_HEADS, ML_DQK), lambda s: (seq(s), 0, 0)),
            pl.BlockSpec((None, ML_HEADS, LANE), lambda s: (seq(s), 0, 0)),
            pl.BlockSpec(memory_space=pl.ANY),
        ],
        out_specs=[
            pl.BlockSpec((rows, D_MLSTM), lambda s: (row_block0 + s, 0)),
            pl.BlockSpec((None, ML_HEADS, ML_DQK, ML_DV), lambda s: (seq(s), 0, 0, 0)),
            pl.BlockSpec((None, ML_HEADS, ML_DQK), lambda s: (seq(s), 0, 0)),
            pl.BlockSpec((None, ML_HEADS, LANE), lambda s: (seq(s), 0, 0)),
        ],
        out_shape=[
            jax.ShapeDtypeStruct((ROWS, D_MLSTM), F32),
            jax.ShapeDtypeStruct((n_seq, ML_HEADS, ML_DQK, ML_DV), F32),
            jax.ShapeDtypeStruct((n_seq, ML_HEADS, ML_DQK), F32),
            jax.ShapeDtypeStruct((n_seq, ML_HEADS, LANE), F32),
        ],
        scratch_shapes=[
            pltpu.VMEM((ML_HEADS, ML_DQK, ML_DV), F32),
            pltpu.VMEM((ML_HEADS, ML_DQK), F32),
            pltpu.VMEM((ML_HEADS, LANE), F32),
        ],
        input_output_aliases={8: 0},
        compiler_params=_cparams(("arbitrary",)),
        name="mlstm_prompt" if carry else "mlstm_sample",
    )(proj, proj, proj, gates, gate_bias, c0, n0, m0, h_init)


def _mix_router_kernel(o_ref, hm_ref, mo_ref, h_ref, wout_ref, dng_ref, mng_ref,
                       lng_ref, lnb_ref, wrh_ref, wrl_ref, br_ref, h2_ref, route_ref, cnt_ref, run_ref):
    @pl.when(pl.program_id(0) == 0)
    def _():
        run_ref[...] = jnp.zeros(run_ref.shape, F32)

    o_da = o_ref[...]
    h_ml = hm_ref[...]
    parts = []
    for hd in range(DA_HEADS):
        seg = o_da[:, hd * DA_DV:(hd + 1) * DA_DV]
        ms = jnp.mean(seg * seg, axis=1, keepdims=True)
        parts.append((seg * lax.rsqrt(ms + LN_EPS) * dng_ref[...] * (1.0 - LAM_INIT)).astype(BF16))
    mo = mo_ref[...]
    for hd in range(ML_HEADS):
        sl = slice(hd * ML_DV, (hd + 1) * ML_DV)
        seg = h_ml[:, sl]
        ms = jnp.mean(seg * seg, axis=1, keepdims=True)
        parts.append((jax.nn.sigmoid(mo[:, sl]) * (seg * lax.rsqrt(ms + LN_EPS) * mng_ref[:, sl])).astype(BF16))
    mix = jnp.dot(jnp.concatenate(parts, axis=1), wout_ref[...], preferred_element_type=F32)
    h2 = _layer_norm(ALPHA * h_ref[...] + mix, lng_ref[...], lnb_ref[...])
    h2_ref[...] = h2

    h2_hi = h2.astype(BF16)
    h2_lo = (h2 - h2_hi.astype(F32)).astype(BF16)
    logits = (jnp.dot(h2_hi, wrh_ref[...], preferred_element_type=F32)
              + (jnp.dot(h2_lo, wrh_ref[...], preferred_element_type=F32)
                 + jnp.dot(h2_hi, wrl_ref[...], preferred_element_type=F32))) + br_ref[...]
    lane = lax.broadcasted_iota(jnp.int32, (MIX_TM, LANE), 1)
    ninf = -jnp.inf
    g_log = jnp.where(lane < N_GROUPS, logits, ninf)
    g_max = jnp.max(g_log, axis=1, keepdims=True)
    g_sel = jnp.min(jnp.where(g_log == g_max, lane, LANE), axis=1, keepdims=True)
    p_sel = 1.0 / jnp.sum(jnp.exp(g_log - g_max), axis=1, keepdims=True)
    lo = N_GROUPS + g_sel * EXPERTS_PER_GROUP
    in_grp = jnp.logical_and(lane >= lo, lane < lo + EXPERTS_PER_GROUP)
    e_log = jnp.where(in_grp, logits, ninf)
    v1 = jnp.max(e_log, axis=1, keepdims=True)
    i1 = jnp.min(jnp.where(e_log == v1, lane, LANE), axis=1, keepdims=True)
    e_log2 = jnp.where(lane == i1, ninf, e_log)
    v2 = jnp.max(e_log2, axis=1, keepdims=True)
    i2 = jnp.min(jnp.where(e_log2 == v2, lane, LANE), axis=1, keepdims=True)
    ex2 = jnp.exp(v2 - v1)
    gate1 = (1.0 / (1.0 + ex2)) * p_sel
    gate2 = (ex2 / (1.0 + ex2)) * p_sel
    e1 = i1 - N_GROUPS
    e2 = i2 - N_GROUPS

    oh1 = (lane == e1).astype(F32)
    oh2 = (lane == e2).astype(F32)
    both = oh1 + oh2
    r = lax.broadcasted_iota(jnp.int32, (MIX_TM, MIX_TM), 0)
    c = lax.broadcasted_iota(jnp.int32, (MIX_TM, MIX_TM), 1)
    strict = (r > c).astype(BF16)
    before = jnp.dot(strict, both.astype(BF16), preferred_element_type=F32) + run_ref[...]
    rank1 = jnp.sum(before * oh1, axis=1, keepdims=True)
    rank2 = jnp.sum(before * oh2, axis=1, keepdims=True)
    run_ref[...] = run_ref[...] + jnp.sum(both, axis=0, keepdims=True)
    cnt_ref[...] = jnp.broadcast_to(run_ref[...], cnt_ref.shape)

    out = jnp.zeros((MIX_TM, LANE), F32)
    for idx, val in enumerate((e1.astype(F32), e2.astype(F32), gate1, gate2, rank1, rank2)):
        out = jnp.where(lane == idx, val, out)
    route_ref[...] = out


def _mix_router(o_da, h_ml, proj, h, wout_bf, dng, mng, lng, lnb, wr_hi, wr_lo, br):
    const = lambda i: (0, 0)
    row = lambda i: (i, 0)
    return pl.pallas_call(
        _mix_router_kernel,
        grid=(ROWS // MIX_TM,),
        in_specs=[
            pl.BlockSpec((MIX_TM, D_DIFF), row),
            pl.BlockSpec((MIX_TM, D_MLSTM), row),
            pl.BlockSpec((MIX_TM, D_MLSTM), lambda i: (i, COL_MO // D_MLSTM)),
            pl.BlockSpec((MIX_TM, D_MODEL), row),
            pl.BlockSpec((D_MODEL, D_MODEL), const),
            pl.BlockSpec((1, DA_DV), const),
            pl.BlockSpec((1, D_MLSTM), const),
            pl.BlockSpec((1, D_MODEL), const),
            pl.BlockSpec((1, D_MODEL), const),
            pl.BlockSpec((D_MODEL, LANE), const),
            pl.BlockSpec((D_MODEL, LANE), const),
            pl.BlockSpec((1, LANE), const),
        ],
        out_specs=[
            pl.BlockSpec((MIX_TM, D_MODEL), row),
            pl.BlockSpec((MIX_TM, LANE), row),
            pl.BlockSpec((8, LANE), const),
        ],
        out_shape=[
            jax.ShapeDtypeStruct((ROWS, D_MODEL), F32),
            jax.ShapeDtypeStruct((ROWS, LANE), F32),
            jax.ShapeDtypeStruct((8, LANE), F32),
        ],
        scratch_shapes=[pltpu.VMEM((1, LANE), F32)],
        compiler_params=_cparams(("arbitrary",)),
        name="mix_router",
    )(o_da, h_ml, proj, h, wout_bf, dng, mng, lng, lnb, wr_hi, wr_lo, br)


def _dispatch_kernel(dest_ref, pend_ref, x_ref, xb_ref, zero_ref, sem, zsem):
    i = pl.program_id(0)
    base = i * (2 * ROW_TILE)

    def zero_block(start):
        return pltpu.make_async_copy(zero_ref, xb_ref.at[pl.ds(pl.multiple_of(start, MOE_BLOCK), MOE_BLOCK)], zsem)

    def zero_copy(e):
        return zero_block(pend_ref[e] - MOE_BLOCK)

    @pl.when(i == 0)
    def _():
        zero_ref[...] = jnp.zeros(zero_ref.shape, F32)
        first_free = pend_ref[N_EXPERTS - 1] // MOE_BLOCK
        for e in range(N_EXPERTS):
            prev = pend_ref[e - 1] if e else 0
            pl.when(pend_ref[e] > prev)(lambda e=e: zero_copy(e).start())
        lax.fori_loop(first_free, N_BLOCKS, lambda b, c: (zero_block(b * MOE_BLOCK).start(), c)[1], 0)
        for e in range(N_EXPERTS):
            prev = pend_ref[e - 1] if e else 0
            pl.when(pend_ref[e] > prev)(lambda e=e: zero_copy(e).wait())
        lax.fori_loop(first_free, N_BLOCKS, lambda b, c: (zero_block(b * MOE_BLOCK).wait(), c)[1], 0)

    def row_copy(r, k):
        return pltpu.make_async_copy(x_ref.at[pl.ds(r, 1)], xb_ref.at[pl.ds(dest_ref[base + 2 * r + k], 1)], sem)

    def issue(r, carry):
        row_copy(r, 0).start()
        row_copy(r, 1).start()
        return carry

    lax.fori_loop(0, ROW_TILE, issue, 0, unroll=DMA_UNROLL)
    for r in range(ROW_TILE):
        row_copy(r, 0).wait()
        row_copy(r, 1).wait()


def _dispatch(dest, pend, h2):
    grid_spec = pltpu.PrefetchScalarGridSpec(
        num_scalar_prefetch=2,
        grid=(N_ROW_TILES,),
        in_specs=[pl.BlockSpec((ROW_TILE, D_MODEL), lambda i, d, p: (i, 0))],
        out_specs=pl.BlockSpec(memory_space=pl.ANY),
        scratch_shapes=[pltpu.VMEM((MOE_BLOCK, D_MODEL), F32), pltpu.SemaphoreType.DMA, pltpu.SemaphoreType.DMA],
    )
    return pl.pallas_call(
        _dispatch_kernel,
        grid_spec=grid_spec,
        out_shape=jax.ShapeDtypeStruct((N_SLOTS, D_MODEL), F32),
        compiler_params=_cparams(("arbitrary",)),
        name="moe_dispatch",
    )(dest, pend, h2)


def _expert_kernel(be_ref, nu_ref, x_ref, wg_ref, wu_ref, wd_ref, y_ref, wg_s, wu_s, wd_s):
    b = pl.program_id(0)
    prev = be_ref[jnp.maximum(b - 1, 0)]

    @pl.when(jnp.logical_or(b == 0, be_ref[b] != prev))
    def _():
        wg_s[...] = wg_ref[...].astype(BF16)
        wu_s[...] = wu_ref[...].astype(BF16)
        wd_s[...] = wd_ref[...].astype(BF16)

    @pl.when(b < nu_ref[0])
    def _():
        xb = x_ref[...].astype(BF16)
        gate = jnp.dot(xb, wg_s[...], preferred_element_type=F32)
        up = jnp.dot(xb, wu_s[...], preferred_element_type=F32)
        hh = (gate * jax.nn.sigmoid(gate)) * up
        y_ref[...] = jnp.dot(hh.astype(BF16), wd_s[...], preferred_element_type=F32)

    @pl.when(b >= nu_ref[0])
    def _():
        y_ref[...] = jnp.zeros(y_ref.shape, F32)


def _experts(block_expert, n_used, xb, w_gate, w_up, w_down):
    grid_spec = pltpu.PrefetchScalarGridSpec(
        num_scalar_prefetch=2,
        grid=(N_BLOCKS,),
        in_specs=[
            pl.BlockSpec((MOE_BLOCK, D_MODEL), lambda b, be, nu: (jnp.minimum(b, nu[0] - 1), 0)),
            pl.BlockSpec((None, None, D_MODEL, D_FF), lambda b, be, nu: (0, be[b], 0, 0)),
            pl.BlockSpec((None, None, D_MODEL, D_FF), lambda b, be, nu: (0, be[b], 0, 0)),
            pl.BlockSpec((None, None, D_FF, D_MODEL), lambda b, be, nu: (0, be[b], 0, 0)),
        ],
        out_specs=pl.BlockSpec((MOE_BLOCK, D_MODEL), lambda b, be, nu: (b, 0)),
        scratch_shapes=[
            pltpu.VMEM((D_MODEL, D_FF), BF16),
            pltpu.VMEM((D_MODEL, D_FF), BF16),
            pltpu.VMEM((D_FF, D_MODEL), BF16),
        ],
    )
    return pl.pallas_call(
        _expert_kernel,
        grid_spec=grid_spec,
        out_shape=jax.ShapeDtypeStruct((N_SLOTS, D_MODEL), F32),
        compiler_params=_cparams(("arbitrary",)),
        name="moe_experts",
    )(block_expert, n_used, xb, w_gate, w_up, w_down)


COMBINE_TILE0 = BLOCK // ROW_TILE
COMBINE_TILES = (SEQ + N_SAMPLE) // ROW_TILE
PROMPT_TILES = SEQ // ROW_TILE


def _combine_kernel(dest_ref, h2_ref, route_ref, g_ref, b_ref, yb_ref, yp_ref, ys_ref, buf, sem):
    i = pl.program_id(0)
    slot = lax.rem(i, 2)

    def row_copy(tile, slot_, r, k):
        src_row = dest_ref[(tile + COMBINE_TILE0) * (2 * ROW_TILE) + 2 * r + k]
        return pltpu.make_async_copy(yb_ref.at[pl.ds(src_row, 1)], buf.at[slot_, k, pl.ds(r, 1)], sem.at[slot_])

    def issue_tile(tile, slot_):
        def body(r, carry):
            row_copy(tile, slot_, r, 0).start()
            row_copy(tile, slot_, r, 1).start()
            return carry
        lax.fori_loop(0, ROW_TILE, body, 0, unroll=DMA_UNROLL)

    @pl.when(i == 0)
    def _():
        issue_tile(i, slot)

    for r in range(ROW_TILE):
        row_copy(i, slot, r, 0).wait()
        row_copy(i, slot, r, 1).wait()

    @pl.when(i + 1 < COMBINE_TILES)
    def _():
        issue_tile(i + 1, 1 - slot)

    route = route_ref[...]
    rows = buf[slot]
    y = route[:, 2:3] * rows[0] + route[:, 3:4] * rows[1]
    out = _layer_norm(ALPHA * h2_ref[...] + y, g_ref[...], b_ref[...])

    @pl.when(i < PROMPT_TILES)
    def _():
        yp_ref[...] = out

    @pl.when(i >= PROMPT_TILES)
    def _():
        ys_ref[...] = out


def _combine(dest, h2, route, g, b, yb):
    tile = lambda i, d: (i + COMBINE_TILE0, 0)
    grid_spec = pltpu.PrefetchScalarGridSpec(
        num_scalar_prefetch=1,
        grid=(COMBINE_TILES,),
        in_specs=[
            pl.BlockSpec((ROW_TILE, D_MODEL), tile),
            pl.BlockSpec((ROW_TILE, LANE), tile),
            pl.BlockSpec((1, D_MODEL), lambda i, d: (0, 0)),
            pl.BlockSpec((1, D_MODEL), lambda i, d: (0, 0)),
            pl.BlockSpec(memory_space=pl.ANY),
        ],
        out_specs=[
            pl.BlockSpec((ROW_TILE, D_MODEL), lambda i, d: (jnp.minimum(i, PROMPT_TILES - 1), 0)),
            pl.BlockSpec((ROW_TILE, D_MODEL), lambda i, d: (jnp.maximum(i - PROMPT_TILES, 0), 0)),
        ],
        scratch_shapes=[pltpu.VMEM((2, 2, ROW_TILE, D_MODEL), F32), pltpu.SemaphoreType.DMA((2,))],
    )
    return pl.pallas_call(
        _combine_kernel,
        grid_spec=grid_spec,
        out_shape=[jax.ShapeDtypeStruct((SEQ, D_MODEL), F32), jax.ShapeDtypeStruct((N_SAMPLE, D_MODEL), F32)],
        compiler_params=_cparams(("arbitrary",)),
        name="moe_combine",
    )(dest, h2, route, g, b, yb)


def _t5_bucket(rel):
    n = np.maximum(rel, 0)
    max_exact = N_BUCKETS // 2
    nf = np.maximum(n, max_exact).astype(np.float32)
    ratio = np.log(nf / np.float32(max_exact)) / np.float32(math.log(MAX_DISTANCE / max_exact))
    large = max_exact + (ratio * np.float32(N_BUCKETS - max_exact)).astype(np.int32)
    large = np.minimum(large, N_BUCKETS - 1)
    return np.where(n < max_exact, n, large)


def _rel_bias(rel_bias, rel):
    table = (rel_bias.astype(F32) - rel_bias.astype(F32)[N_BUCKETS - 1][None, :]) * LOG2E
    onehot = _t5_bucket(rel)[None, ..., None] == np.arange(N_BUCKETS)
    shaped = table.T.reshape((DA_HEADS,) + (1,) * rel.ndim + (N_BUCKETS,))
    return jnp.sum(jnp.where(onehot, shaped, 0.0), axis=-1)


def kernel(x_prompt, x_sample, cache_k, cache_v, state_C, state_n, state_m, page_table, meta_tokens, ln_in_g, ln_in_b, rel_bias, w_in, b_gates, lambda_q1, lambda_k1, lambda_q2, lambda_k2, diff_norm_g, mlstm_norm_g, w_out, ln_mix_g, ln_mix_b, w_router_g, b_router_g, w_router_e, b_router_e, w_gate, w_up, w_down, ln_ffn_g, ln_ffn_b):
    lam = (jnp.exp(jnp.sum(lambda_q1[0].astype(F32) * lambda_k1[0].astype(F32)))
           - jnp.exp(jnp.sum(lambda_q2[0].astype(F32) * lambda_k2[0].astype(F32))) + LAM_INIT).reshape(1)

    x = jnp.concatenate([
        jnp.zeros((META_PAD, D_MODEL), F32), meta_tokens.astype(F32), x_prompt.reshape(SEQ, D_MODEL),
        x_sample.reshape(N_SAMPLE, D_MODEL), jnp.zeros((ROWS - LP - N_SAMPLE, D_MODEL), F32)], axis=0)
    w_in_bf = w_in[0].astype(BF16)
    w_gate_cols = jnp.pad(w_in_bf[:, D_MAIN:], ((0, 0), (0, LANE - (D_IN - D_MAIN))))
    h, proj, gates, qlo, qhi, kbf, vbf = _ln_proj(
        x, ln_in_g.reshape(1, D_MODEL), ln_in_b.reshape(1, D_MODEL), w_in_bf[:, :D_MAIN], w_gate_cols)

    ar = np.arange(LANE)
    rel0 = ar[None, :] - ar[:, None]
    tiles = jnp.stack([_rel_bias(rel_bias, rel0), _rel_bias(rel_bias, rel0 + LANE)], axis=1)
    o_da = _prompt_attention(qlo, qhi, kbf, vbf, tiles, lam, jnp.zeros((ROWS, D_DIFF), F32))

    smp = slice(ROW_SAMPLE, ROW_SAMPLE + N_SAMPLE)
    q_s = proj[smp, :D_DIFF].reshape(DEC_BATCH, DEC_SEQ, DA_HEADS, 2, DA_DH) * Q_SCALE
    q5 = jnp.transpose(q_s, (0, 2, 3, 1, 4))
    eye_c = jnp.eye(2, dtype=F32)
    qt = (q5[:, :, :, :, None, :] * eye_c[None, None, :, None, :, None]).reshape(DEC_BATCH, SA_COLS, DA_DV).astype(BF16)
    zrows = jnp.zeros((DEC_BATCH, LANE - DEC_SEQ * DA_HEADS, DA_DV), F32)
    knew = jnp.concatenate([proj[smp, COL_K:COL_K + D_DIFF].reshape(DEC_BATCH, DEC_SEQ * DA_HEADS, DA_DV), zrows], axis=1)
    vnew = jnp.concatenate([proj[smp, COL_V:COL_V + D_DIFF].reshape(DEC_BATCH, DEC_SEQ * DA_HEADS, DA_DV), zrows], axis=1)
    qpos = np.arange(DEC_SEQ)
    tpos = np.arange(PAGE_SIZE)
    same_head = jnp.eye(DA_HEADS, dtype=bool)

    def score_bias(b_hqt, ok_qt):
        n_t = b_hqt.shape[-1]
        full = jnp.where(same_head[:, None, None, None, :] & ok_qt[None, None, :, :, None],
                         b_hqt[:, None, :, :, None], NEG)
        return jnp.broadcast_to(full, (DA_HEADS, 2, DEC_SEQ, n_t, DA_HEADS)).reshape(SA_COLS, n_t * DA_HEADS)

    all_ok = jnp.ones((DEC_SEQ, PAGE_SIZE), bool)
    b_last = _rel_bias(rel_bias, PAGE_SIZE + qpos[:, None] - tpos[None, :])
    bm = jnp.stack([score_bias(jnp.zeros_like(b_last), all_ok), score_bias(b_last, all_ok)])
    rel_new = qpos[:, None] - qpos[None, :]
    bmnew = jnp.concatenate([score_bias(_rel_bias(rel_bias, rel_new), rel_new >= 0),
                             jnp.full((SA_COLS, LANE - DEC_SEQ * DA_HEADS), NEG, F32)], axis=1)
    bmnew = jnp.stack([jnp.full((SA_COLS, LANE), NEG, F32), bmnew])
    o_da = _sample_attention(qt, knew, vnew, bm, bmnew, cache_k, cache_v, page_table.astype(jnp.int32), lam, o_da)

    gate_bias = jnp.pad(b_gates[0].astype(F32), (0, LANE - 2 * ML_HEADS)).reshape(1, LANE)
    h_ml, c_p, n_p, m_p = _mlstm(
        proj, gates, gate_bias, jnp.zeros((1, ML_HEADS, ML_DQK, ML_DV), F32),
        jnp.zeros((1, ML_HEADS, ML_DQK), F32), jnp.zeros((1, ML_HEADS, LANE), F32),
        jnp.zeros((ROWS, D_MLSTM), F32), n_seq=1, rows=BLOCK, carry=True, row_block0=0,
        n_steps=N_CHUNKS, pad_front=META_PAD)
    h_ml, c_s, n_s, m_s = _mlstm(
        proj, gates, gate_bias, state_C[0].astype(F32), state_n[0].astype(F32),
        jnp.broadcast_to(state_m[0].astype(F32)[:, :, None], (DEC_BATCH, ML_HEADS, LANE)), h_ml,
        n_seq=DEC_BATCH, rows=DEC_SEQ, carry=False, row_block0=ROW_SAMPLE // DEC_SEQ,
        n_steps=DEC_BATCH, pad_front=0)

    wr = jnp.pad(jnp.concatenate([w_router_g[0], w_router_e[0]], axis=1).astype(F32),
                 ((0, 0), (0, LANE - N_GROUPS - N_EXPERTS)))
    wr_hi = wr.astype(BF16)
    wr_lo = (wr - wr_hi.astype(F32)).astype(BF16)
    br = jnp.pad(jnp.concatenate([b_router_g[0], b_router_e[0]]).astype(F32),
                 (0, LANE - N_GROUPS - N_EXPERTS)).reshape(1, LANE)
    h2, route, counts = _mix_router(
        o_da, h_ml, proj, h, w_out[0].astype(BF16), diff_norm_g[0].reshape(1, DA_DV),
        mlstm_norm_g[0].reshape(1, D_MLSTM), ln_mix_g[0].reshape(1, D_MODEL), ln_mix_b[0].reshape(1, D_MODEL),
        wr_hi, wr_lo, br)

    cnt = counts[0, :N_EXPERTS].astype(jnp.int32)
    padded = (cnt + MOE_BLOCK - 1) // MOE_BLOCK * MOE_BLOCK
    pend = jnp.cumsum(padded)
    pstart = pend - padded
    e12 = route[:, 0:2].astype(jnp.int32)
    experts = jnp.arange(N_EXPERTS, dtype=jnp.int32)
    slot0 = jnp.sum(jnp.where(e12[:, :, None] == experts, pstart, 0), axis=-1)
    dest = (slot0 + route[:, 4:6].astype(jnp.int32)).reshape(N_ASSIGN)
    block_start = jnp.arange(N_BLOCKS, dtype=jnp.int32) * MOE_BLOCK
    block_expert = jnp.minimum(jnp.sum(block_start[:, None] >= pend[None, :], axis=1), N_EXPERTS - 1).astype(jnp.int32)
    n_used = (pend[-1] // MOE_BLOCK).astype(jnp.int32).reshape(1)

    xb = _dispatch(dest, pend.astype(jnp.int32), h2)
    yb = _experts(block_expert, n_used, xb, w_gate, w_up, w_down)
    y_p, y_s = _combine(dest, h2, route, ln_ffn_g[0].reshape(1, D_MODEL), ln_ffn_b[0].reshape(1, D_MODEL), yb)

    lq = N_META + SEQ
    y_prompt = y_p.reshape(1, SEQ, D_MODEL)
    y_sample = y_s.reshape(DEC_BATCH, DEC_SEQ, D_MODEL)
    k_prompt = proj[META_PAD:LP, COL_K:COL_K + D_DIFF].reshape(1, 1, lq, DA_HEADS, 2 * DA_DH)
    v_prompt = proj[META_PAD:LP, COL_V:COL_V + D_DIFF].reshape(1, 1, lq, DA_HEADS, DA_DV)
    k_sample = proj[smp, COL_K:COL_K + D_DIFF].reshape(1, DEC_BATCH, DEC_SEQ, DA_HEADS, 2 * DA_DH)
    v_sample = proj[smp, COL_V:COL_V + D_DIFF].reshape(1, DEC_BATCH, DEC_SEQ, DA_HEADS, DA_DV)
    return (y_prompt, y_sample, k_prompt, v_prompt,
            c_p[None], n_p[None], m_p[None, :, :, 0],
            k_sample, v_sample, c_s[None], n_s[None], m_s[None, :, :, 0])
```

```python
import functools
import math

import numpy as np
import jax
import jax.numpy as jnp
from jax import lax
from jax.experimental import pallas as pl
from jax.experimental.pallas import tpu as pltpu

F32 = jnp.float32
BF16 = jnp.bfloat16

D_MODEL = 2048
SEQ = 8192
DEC_BATCH = 32
DEC_SEQ = 8
PAGE_SIZE = 128
N_PAGES = 128
N_META = 16
BLOCK = 128
META_PAD = 112
DA_HEADS = 8
DA_DH = 64
DA_DV = 128
D_DIFF = 1024
ML_HEADS = 4
ML_DQK = 128
ML_DV = 256
D_MLSTM = 1024
D_IN = 6152
N_BUCKETS = 32
MAX_DISTANCE = 128
N_GROUPS = 4
EXPERTS_PER_GROUP = 8
N_EXPERTS = 32
D_FF = 512
ALPHA = 2.0 ** 0.25
LN_EPS = 1e-5
NEG = -1e30
LAM_INIT = 0.8 - 0.6 * math.exp(-0.0)
LOG2E = math.log2(math.e)
Q_SCALE = DA_DH ** -0.5 * LOG2E

LANE = 128
LP = META_PAD + N_META + SEQ
N_CHUNKS = LP // BLOCK
N_SAMPLE = DEC_BATCH * DEC_SEQ
ROW_SAMPLE = LP
ROWS = LP + N_SAMPLE + BLOCK
D_MAIN = 6144
COL_K = 1024
COL_V = 2048
COL_MQ = 3072
COL_MK = 3584
COL_MV = 4096
COL_MO = 5120

PROJ_TM = 512
PROJ_TN = 1024
ATT_T = 640
ATT_NB = LP // ATT_T
ATT_HEADS = 4
ATT_ONES = 16
PAGES_PER_STEP = 8
PAGE_LOOKAHEAD = 2
PAGE_SLOTS = PAGE_LOOKAHEAD + 1
SA_STEPS = N_PAGES // PAGES_PER_STEP
PAGE_ROWS = PAGE_SIZE * DA_HEADS
SA_COLS = DA_HEADS * 2 * DEC_SEQ
MOE_BLOCK = 256
N_ASSIGN = 2 * ROWS
N_BLOCKS = N_ASSIGN // MOE_BLOCK + N_EXPERTS
N_SLOTS = N_BLOCKS * MOE_BLOCK
ROW_TILE = 128
N_ROW_TILES = ROWS // ROW_TILE
MIX_TM = 512
DMA_UNROLL = 8

VMEM_LIMIT = 56 * 1024 * 1024


def _cparams(sem, vmem=VMEM_LIMIT):
    return pltpu.CompilerParams(dimension_semantics=sem, vmem_limit_bytes=vmem)


def _layer_norm(x, g, b):
    mu = jnp.mean(x, axis=-1, keepdims=True)
    xc = x - mu
    var = jnp.mean(xc * xc, axis=-1, keepdims=True)
    return xc * lax.rsqrt(var + LN_EPS) * g + b


_NT = (((1,), (1,)), ((), ()))


def _ln_proj_kernel(x_ref, g_ref, b_ref, w_ref, wg_ref, h_ref, proj_ref, gate_ref,
                    qlo_ref, qhi_ref, kb_ref, vb_ref, xn_ref):
    j = pl.program_id(1)

    @pl.when(j == 0)
    def _():
        hn = _layer_norm(x_ref[...], g_ref[...], b_ref[...])
        h_ref[...] = hn
        xn = hn.astype(BF16)
        xn_ref[...] = xn
        gate_ref[...] = jnp.dot(xn, wg_ref[...], preferred_element_type=F32)

    acc = jnp.dot(xn_ref[...], w_ref[...], preferred_element_type=F32)
    proj_ref[...] = acc

    @pl.when(j == 0)
    def _():
        lane = lax.broadcasted_iota(jnp.int32, acc.shape, 1)
        first = (lane & (DA_DV - 1)) < DA_DH
        qs = acc * Q_SCALE
        qlo_ref[...] = jnp.where(first, qs, 0.0).astype(BF16)
        qhi_ref[...] = jnp.where(first, 0.0, qs).astype(BF16)

    @pl.when(j == 1)
    def _():
        kb_ref[...] = acc.astype(BF16)

    @pl.when(j == 2)
    def _():
        vb_ref[...] = acc.T.astype(BF16)


def _ln_proj(x, g, b, w_bf, wg_bf):
    row = lambda i, j: (i, 0)
    bf_shape = jax.ShapeDtypeStruct((ROWS, D_DIFF), BF16)
    return pl.pallas_call(
        _ln_proj_kernel,
        grid=(ROWS // PROJ_TM, D_MAIN // PROJ_TN),
        in_specs=[
            pl.BlockSpec((PROJ_TM, D_MODEL), row),
            pl.BlockSpec((1, D_MODEL), lambda i, j: (0, 0)),
            pl.BlockSpec((1, D_MODEL), lambda i, j: (0, 0)),
            pl.BlockSpec((D_MODEL, PROJ_TN), lambda i, j: (0, j)),
            pl.BlockSpec((D_MODEL, LANE), lambda i, j: (0, 0)),
        ],
        out_specs=[
            pl.BlockSpec((PROJ_TM, D_MODEL), row),
            pl.BlockSpec((PROJ_TM, PROJ_TN), lambda i, j: (i, j)),
            pl.BlockSpec((PROJ_TM, LANE), row),
            pl.BlockSpec((PROJ_TM, D_DIFF), row),
            pl.BlockSpec((PROJ_TM, D_DIFF), row),
            pl.BlockSpec((PROJ_TM, D_DIFF), row),
            pl.BlockSpec((D_DIFF, PROJ_TM), lambda i, j: (0, i)),
        ],
        out_shape=[
            jax.ShapeDtypeStruct((ROWS, D_MODEL), F32),
            jax.ShapeDtypeStruct((ROWS, D_MAIN), F32),
            jax.ShapeDtypeStruct((ROWS, LANE), F32),
            bf_shape, bf_shape, bf_shape,
            jax.ShapeDtypeStruct((D_DIFF, ROWS), BF16),
        ],
        scratch_shapes=[pltpu.VMEM((PROJ_TM, D_MODEL), BF16)],
        compiler_params=_cparams(("arbitrary", "arbitrary")),
        name="ln_proj",
    )(x, g, b, w_bf, wg_bf)


def _attn_kernel(qi_ref, kj_ref, lam_ref, qlo_ref, qhi_ref, k_ref, vt_ref, t_ref, o_init_ref, o_ref,
                 bd_ref, bs_ref, *state_refs):
    m_refs = state_refs[:2 * ATT_HEADS]
    acc_refs = state_refs[2 * ATT_HEADS:]
    del o_init_ref
    t = pl.program_id(1)
    i = qi_ref[t]
    j = kj_ref[t]
    nsub = ATT_T // LANE

    @pl.when(t == 0)
    def _():
        z = jnp.zeros((LANE, LANE), F32)
        for a in range(ATT_HEADS):
            t0 = t_ref[a, 0]
            t1 = t_ref[a, 1]
            for r in range(nsub):
                bd_ref[a, r * LANE:(r + 1) * LANE, :] = jnp.concatenate(
                    [t0 if c == r else (t1 if c == r + 1 else z) for c in range(nsub)], axis=1)
                bs_ref[a, r * LANE:(r + 1) * LANE, :] = jnp.concatenate(
                    [t1 if (r == nsub - 1 and c == 0) else z for c in range(nsub)], axis=1)

    @pl.when(j == 0)
    def _():
        for m_ref, acc_ref in zip(m_refs, acc_refs):
            m_ref[...] = jnp.full(m_ref.shape, -jnp.inf, F32)
            acc_ref[...] = jnp.zeros(acc_ref.shape, F32)

    ones_rows = jnp.ones((ATT_ONES, ATT_T), BF16)

    def update(bias_ref, causal, padmask):
        if causal or padmask:
            r = lax.broadcasted_iota(jnp.int32, (ATT_T, ATT_T), 0)
            c = lax.broadcasted_iota(jnp.int32, (ATT_T, ATT_T), 1)
            valid = (r + j * ATT_T) >= META_PAD
            if causal:
                valid = jnp.logical_and(valid, r <= c)
        def score(idx):
            a, comp = divmod(idx, 2)
            sl = slice(a * DA_DV, (a + 1) * DA_DV)
            q_ref = (qlo_ref, qhi_ref)[comp]
            s = lax.dot_general(k_ref[:, sl], q_ref[:, sl], _NT, preferred_element_type=F32)
            if bias_ref is not None:
                s = s + bias_ref[a]
            if causal or padmask:
                s = jnp.where(valid, s, NEG)
            return s

        def accumulate(idx, s, m_prev, m_new):
            a = idx // 2
            vt = jnp.concatenate([vt_ref[a * DA_DV:(a + 1) * DA_DV, :], ones_rows], axis=0)
            acc_ref = acc_refs[idx]
            alpha = jnp.exp2(m_prev - m_new)
            p = jnp.exp2(s - m_new).astype(BF16)
            acc_ref[...] = alpha * acc_ref[...] + jnp.dot(vt, p, preferred_element_type=F32)

        n_chain = 2 * ATT_HEADS
        scores = [score(idx) for idx in range(n_chain)]
        m_prevs = [m_refs[idx][...] for idx in range(n_chain)]
        m_news = [jnp.maximum(mp, jnp.max(s, axis=0, keepdims=True)) for mp, s in zip(m_prevs, scores)]
        for idx in range(n_chain):
            m_refs[idx][...] = m_news[idx]
        for idx in range(n_chain):
            accumulate(idx, scores[idx], m_prevs[idx], m_news[idx])

    @pl.when(j == i)
    def _():
        update(bd_ref, True, True)
        for a in range(ATT_HEADS):
            n1 = acc_refs[2 * a][...]
            n2 = acc_refs[2 * a + 1][...]
            ot = (n1[:DA_DV] / n1[DA_DV:DA_DV + 1] - lam_ref[0] * (n2[:DA_DV] / n2[DA_DV:DA_DV + 1]))
            o_ref[:, a * DA_DV:(a + 1) * DA_DV] = ot.T

    @pl.when(j == i - 1)
    def _():
        update(bs_ref, False, True)

    @pl.when(jnp.logical_and(j == 0, i >= 2))
    def _():
        update(None, False, True)

    @pl.when(jnp.logical_and(j >= 1, j <= i - 2))
    def _():
        update(None, False, False)


def _prompt_attention(qlo, qhi, kb, vt, tiles, lam, o_init):
    pairs = [(i, j) for i in range(ATT_NB) for j in range(i + 1)]
    qi = jnp.asarray([p[0] for p in pairs], jnp.int32)
    kj = jnp.asarray([p[1] for p in pairs], jnp.int32)
    width = ATT_HEADS * DA_DV
    qmap = lambda h, t, qi, kj: (qi[t], h)
    kmap = lambda h, t, qi, kj: (kj[t], h)
    grid_spec = pltpu.PrefetchScalarGridSpec(
        num_scalar_prefetch=2,
        grid=(DA_HEADS // ATT_HEADS, len(pairs)),
        in_specs=[
            pl.BlockSpec(memory_space=pltpu.SMEM),
            pl.BlockSpec((ATT_T, width), qmap),
            pl.BlockSpec((ATT_T, width), qmap),
            pl.BlockSpec((ATT_T, width), kmap),
            pl.BlockSpec((width, ATT_T), lambda h, t, qi, kj: (h, kj[t])),
            pl.BlockSpec((ATT_HEADS, 2, LANE, LANE), lambda h, t, qi, kj: (h, 0, 0, 0)),
            pl.BlockSpec(memory_space=pl.ANY),
        ],
        out_specs=pl.BlockSpec((ATT_T, width), qmap),
        scratch_shapes=[
            pltpu.VMEM((ATT_HEADS, ATT_T, ATT_T), F32),
            pltpu.VMEM((ATT_HEADS, ATT_T, ATT_T), F32),
        ] + [pltpu.VMEM((1, ATT_T), F32)] * (2 * ATT_HEADS) + [
            pltpu.VMEM((DA_DV + ATT_ONES, ATT_T), F32)] * (2 * ATT_HEADS) + [
        ],
    )
    return pl.pallas_call(
        _attn_kernel,
        grid_spec=grid_spec,
        out_shape=jax.ShapeDtypeStruct((ROWS, D_DIFF), F32),
        input_output_aliases={8: 0},
        compiler_params=_cparams(("arbitrary", "arbitrary")),
        name="prompt_attn",
    )(qi, kj, lam, qlo, qhi, kb, vt, tiles, o_init)


def _sattn_kernel(pt_ref, lam_ref, qt_ref, knew_ref, vnew_ref, bm_ref, bmnew_ref, ck_ref, cv_ref, o_init_ref,
                  o_ref, m_ref, l_ref, acc_ref, kbuf, vbuf, sem):
    del o_init_ref
    b = pl.program_id(0)
    g = pl.program_id(1)
    step = b * SA_STEPS + g
    n_steps = DEC_BATCH * SA_STEPS
    last = g == SA_STEPS - 1

    def page_copies(step_, slot_):
        seq = step_ // SA_STEPS
        first = (step_ - seq * SA_STEPS) * PAGES_PER_STEP
        copies = []
        for r in range(PAGES_PER_STEP):
            page = pt_ref[seq, first + r]
            copies.append(pltpu.make_async_copy(ck_ref.at[page], kbuf.at[slot_, r], sem.at[slot_]))
            copies.append(pltpu.make_async_copy(cv_ref.at[page], vbuf.at[slot_, r], sem.at[slot_]))
        return copies

    def fetch(step_):
        for c in page_copies(step_, lax.rem(step_, PAGE_SLOTS)):
            c.start()

    @pl.when(step == 0)
    def _():
        for ahead in range(PAGE_LOOKAHEAD):
            fetch(step + ahead)

    @pl.when(step + PAGE_LOOKAHEAD < n_steps)
    def _():
        fetch(step + PAGE_LOOKAHEAD)

    slot = lax.rem(step, PAGE_SLOTS)
    for c in page_copies(step, slot):
        c.wait()

    @pl.when(g == 0)
    def _():
        m_ref[...] = jnp.full(m_ref.shape, -jnp.inf, F32)
        l_ref[...] = jnp.zeros(l_ref.shape, F32)
        acc_ref[...] = jnp.zeros(acc_ref.shape, F32)

    qt = qt_ref[...]
    is_last = last.astype(jnp.int32)

    k_blocks = [kbuf[slot, r] for r in range(PAGES_PER_STEP)] + [knew_ref[...]]
    v_blocks = [vbuf[slot, r] for r in range(PAGES_PER_STEP)] + [vnew_ref[...]]
    biases = [bm_ref[0]] * (PAGES_PER_STEP - 1) + [bm_ref[is_last], bmnew_ref[is_last]]
    scores = [lax.dot_general(qt, kr.astype(BF16), _NT, preferred_element_type=F32) + b
              for kr, b in zip(k_blocks, biases)]
    m_prev = m_ref[...]
    m_new = m_prev
    for s in scores:
        m_new = jnp.maximum(m_new, jnp.max(s, axis=1, keepdims=True))
    alpha = jnp.exp2(m_prev - m_new)
    l_new = alpha * l_ref[...]
    acc_new = alpha * acc_ref[...]
    for s, vr in zip(scores, v_blocks):
        p = jnp.exp2(s - m_new)
        l_new = l_new + jnp.sum(p, axis=1, keepdims=True)
        acc_new = acc_new + jnp.dot(p.astype(BF16), vr.astype(BF16), preferred_element_type=F32)
    m_ref[...] = m_new
    l_ref[...] = l_new
    acc_ref[...] = acc_new

    @pl.when(last)
    def _():
        oc = acc_new / l_new
        lam = lam_ref[0]
        rows_per_head = 2 * DEC_SEQ
        for h in range(DA_HEADS):
            r0 = h * rows_per_head
            o_ref[:, h * DA_DV:(h + 1) * DA_DV] = oc[r0:r0 + DEC_SEQ] - lam * oc[r0 + DEC_SEQ:r0 + 2 * DEC_SEQ]


def _sample_attention(qt, knew, vnew, bm, bmnew, cache_k, cache_v, page_table, lam, o_init):
    per_seq = lambda b, g, pt: (b, 0, 0)
    any_spec = pl.BlockSpec(memory_space=pl.ANY)
    grid_spec = pltpu.PrefetchScalarGridSpec(
        num_scalar_prefetch=1,
        grid=(DEC_BATCH, SA_STEPS),
        in_specs=[
            pl.BlockSpec(memory_space=pltpu.SMEM),
            pl.BlockSpec((None, SA_COLS, DA_DV), per_seq),
            pl.BlockSpec((None, LANE, DA_DV), per_seq),
            pl.BlockSpec((None, LANE, DA_DV), per_seq),
            pl.BlockSpec((2, SA_COLS, PAGE_ROWS), lambda b, g, pt: (0, 0, 0)),
            pl.BlockSpec((2, SA_COLS, LANE), lambda b, g, pt: (0, 0, 0)),
            any_spec, any_spec, any_spec,
        ],
        out_specs=pl.BlockSpec((DEC_SEQ, D_DIFF), lambda b, g, pt: (ROW_SAMPLE // DEC_SEQ + b, 0)),
        scratch_shapes=[
            pltpu.VMEM((SA_COLS, 1), F32),
            pltpu.VMEM((SA_COLS, 1), F32),
            pltpu.VMEM((SA_COLS, DA_DV), F32),
            pltpu.VMEM((PAGE_SLOTS, PAGES_PER_STEP, PAGE_ROWS, DA_DV), F32),
            pltpu.VMEM((PAGE_SLOTS, PAGES_PER_STEP, PAGE_ROWS, DA_DV), F32),
            pltpu.SemaphoreType.DMA((PAGE_SLOTS,)),
        ],
    )
    ck = cache_k.reshape(cache_k.shape[1], PAGE_ROWS, DA_DV)
    cv = cache_v.reshape(cache_v.shape[1], PAGE_ROWS, DA_DV)
    return pl.pallas_call(
        _sattn_kernel,
        grid_spec=grid_spec,
        out_shape=jax.ShapeDtypeStruct((ROWS, D_DIFF), F32),
        input_output_aliases={9: 0},
        compiler_params=_cparams(("arbitrary", "arbitrary")),
        name="sample_attn",
    )(page_table, lam, qt, knew, vnew, bm, bmnew, ck, cv, o_init)


def _mlstm_kernel(q_ref, k_ref, v_ref, g_ref, gb_ref, c0_ref, n0_ref, m0_ref, h_init_ref,
                  h_ref, c_ref, n_ref, m_ref, cs, ns, ms, *, rows, carry, pad_front):
    del h_init_ref
    step = pl.program_id(0)
    L = BLOCK

    def load_state():
        cs[...] = c0_ref[...]
        ns[...] = n0_ref[...]
        ms[...] = m0_ref[...]

    if carry:
        pl.when(step == 0)(load_state)
    else:
        load_state()

    def pad_rows(x):
        if rows == L:
            return x
        return jnp.concatenate([x, jnp.zeros((L - rows, x.shape[1]), x.dtype)], axis=0)

    row = lax.broadcasted_iota(jnp.int32, (L, L), 0)
    col = lax.broadcasted_iota(jnp.int32, (L, L), 1)
    not_token = row >= rows
    if pad_front:
        not_token = jnp.logical_or(not_token, row + step * L < pad_front)
    pre = pad_rows(g_ref[...]) + gb_ref[...]
    logi = jnp.where(not_token, NEG, pre)
    logf = jnp.where(not_token, 0.0, jnp.minimum(pre, 0.0) - jnp.log1p(jnp.exp(-jnp.abs(pre))))
    tril = (row >= col).astype(F32)
    bcum = jnp.dot(tril, logf, preferred_element_type=F32, precision=lax.Precision.HIGHEST)
    bcum_t = bcum.T
    logi_t = logi.T

    qa = pad_rows(q_ref[...]) * (ML_DQK ** -0.5)
    ka = pad_rows(k_ref[...])
    va = pad_rows(v_ref[...])
    for h in range(ML_HEADS):
        bc = bcum[:, ML_HEADS + h:ML_HEADS + h + 1]
        br = bcum_t[ML_HEADS + h:ML_HEADS + h + 1, :]
        ic = logi[:, h:h + 1]
        ir = logi_t[h:h + 1, :]
        d = jnp.where(row >= col, (bc - br) + ir, -jnp.inf)
        m_prev = ms[h:h + 1, 0:1]
        m_inter = m_prev + bc
        m_t = jnp.maximum(m_inter, jnp.max(d, axis=1, keepdims=True))
        w = jnp.exp(d - m_t)
        qh = qa[:, h * ML_DQK:(h + 1) * ML_DQK]
        kh = ka[:, h * ML_DQK:(h + 1) * ML_DQK]
        qb = qh.astype(BF16)
        vb = va[:, h * ML_DV:(h + 1) * ML_DV].astype(BF16)
        qk = lax.dot_general(qb, kh.astype(BF16), _NT, preferred_element_type=F32)
        s = w * qk
        inter = jnp.exp(m_inter - m_t)
        c_old = cs[h]
        n_old = ns[h:h + 1, :]
        num = (jnp.dot(s.astype(BF16), vb, preferred_element_type=F32)
               + inter * jnp.dot(qb, c_old.astype(BF16), preferred_element_type=F32))
        den = jnp.sum(s, axis=1, keepdims=True) + inter * jnp.sum(qh * n_old, axis=1, keepdims=True)
        hh = num / jnp.maximum(jnp.abs(den), jnp.exp(-m_t))
        h_ref[:, h * ML_DV:(h + 1) * ML_DV] = hh[:rows]
        m_new = m_t[L - 1:L, :]
        w_end = jnp.exp(bc[L - 1:L, :] - bc + ic - m_new)
        decay = jnp.exp(m_inter[L - 1:L, :] - m_new)
        kw = kh * w_end
        cs[h] = decay * c_old + jnp.dot(kw.T.astype(BF16), vb, preferred_element_type=F32)
        ns[h:h + 1, :] = decay * n_old + jnp.sum(kw, axis=0, keepdims=True)
        ms[h:h + 1, :] = jnp.broadcast_to(m_new, (1, LANE))

    c_ref[...] = cs[...]
    n_ref[...] = ns[...]
    m_ref[...] = ms[...]


def _mlstm(proj, gates, gate_bias, c0, n0, m0, h_init, *, n_seq, rows, carry, row_block0, n_steps, pad_front):
    seq = (lambda s: 0) if carry else (lambda s: s)
    kernel = functools.partial(_mlstm_kernel, rows=rows, carry=carry, pad_front=pad_front)
    dqk = ML_HEADS * ML_DQK
    return pl.pallas_call(
        kernel,
        grid=(n_steps,),
        in_specs=[
            pl.BlockSpec((rows, dqk), lambda s: (row_block0 + s, COL_MQ // dqk)),
            pl.BlockSpec((rows, dqk), lambda s: (row_block0 + s, COL_MK // dqk)),
            pl.BlockSpec((rows, D_MLSTM), lambda s: (row_block0 + s, COL_MV // D_MLSTM)),
            pl.BlockSpec((rows, LANE), lambda s: (row_block0 + s, 0)),
            pl.BlockSpec((1, LANE), lambda s: (0, 0)),
            pl.BlockSpec((None, ML_HEADS, ML_DQK, ML_DV), lambda s: (seq(s), 0, 0, 0)),
            pl.BlockSpec((None, ML_HEADS, ML_DQK), lambda s: (seq(s), 0, 0)),
            pl.BlockSpec((None, ML_HEADS, LANE), lambda s: (seq(s), 0, 0)),
            pl.BlockSpec(memory_space=pl.ANY),
        ],
        out_specs=[
            pl.BlockSpec((rows, D_MLSTM), lambda s: (row_block0 + s, 0)),
            pl.BlockSpec((None, ML_HEADS, ML_DQK, ML_DV), lambda s: (seq(s), 0, 0, 0)),
            pl.BlockSpec((None, ML_HEADS, ML_DQK), lambda s: (seq(s), 0, 0)),
            pl.BlockSpec((None, ML_HEADS, LANE), lambda s: (seq(s), 0, 0)),
        ],
        out_shape=[
            jax.ShapeDtypeStruct((ROWS, D_MLSTM), F32),
            jax.ShapeDtypeStruct((n_seq, ML_HEADS, ML_DQK, ML_DV), F32),
            jax.ShapeDtypeStruct((n_seq, ML_HEADS, ML_DQK), F32),
            jax.ShapeDtypeStruct((n_seq, ML_HEADS, LANE), F32),
        ],
        scratch_shapes=[
            pltpu.VMEM((ML_HEADS, ML_DQK, ML_DV), F32),
            pltpu.VMEM((ML_HEADS, ML_DQK), F32),
            pltpu.VMEM((ML_HEADS, LANE), F32),
        ],
        input_output_aliases={8: 0},
        compiler_params=_cparams(("arbitrary",)),
        name="mlstm_prompt" if carry else "mlstm_sample",
    )(proj, proj, proj, gates, gate_bias, c0, n0, m0, h_init)


def _mix_router_kernel(o_ref, hm_ref, mo_ref, h_ref, wout_ref, dng_ref, mng_ref,
                       lng_ref, lnb_ref, wrh_ref, wrl_ref, br_ref, h2_ref, route_ref, cnt_ref, run_ref):
    @pl.when(pl.program_id(0) == 0)
    def _():
        run_ref[...] = jnp.zeros(run_ref.shape, F32)

    o_da = o_ref[...]
    h_ml = hm_ref[...]
    parts = []
    for hd in range(DA_HEADS):
        seg = o_da[:, hd * DA_DV:(hd + 1) * DA_DV]
        ms = jnp.mean(seg * seg, axis=1, keepdims=True)
        parts.append((seg * lax.rsqrt(ms + LN_EPS) * dng_ref[...] * (1.0 - LAM_INIT)).astype(BF16))
    mo = mo_ref[...]
    for hd in range(ML_HEADS):
        sl = slice(hd * ML_DV, (hd + 1) * ML_DV)
        seg = h_ml[:, sl]
        ms = jnp.mean(seg * seg, axis=1, keepdims=True)
        parts.append((jax.nn.sigmoid(mo[:, sl]) * (seg * lax.rsqrt(ms + LN_EPS) * mng_ref[:, sl])).astype(BF16))
    mix = jnp.dot(jnp.concatenate(parts, axis=1), wout_ref[...], preferred_element_type=F32)
    h2 = _layer_norm(ALPHA * h_ref[...] + mix, lng_ref[...], lnb_ref[...])
    h2_ref[...] = h2

    h2_hi = h2.astype(BF16)
    h2_lo = (h2 - h2_hi.astype(F32)).astype(BF16)
    logits = (jnp.dot(h2_hi, wrh_ref[...], preferred_element_type=F32)
              + (jnp.dot(h2_lo, wrh_ref[...], preferred_element_type=F32)
                 + jnp.dot(h2_hi, wrl_ref[...], preferred_element_type=F32))) + br_ref[...]
    lane = lax.broadcasted_iota(jnp.int32, (MIX_TM, LANE), 1)
    ninf = -jnp.inf
    g_log = jnp.where(lane < N_GROUPS, logits, ninf)
    g_max = jnp.max(g_log, axis=1, keepdims=True)
    g_sel = jnp.min(jnp.where(g_log == g_max, lane, LANE), axis=1, keepdims=True)
    p_sel = 1.0 / jnp.sum(jnp.exp(g_log - g_max), axis=1, keepdims=True)
    lo = N_GROUPS + g_sel * EXPERTS_PER_GROUP
    in_grp = jnp.logical_and(lane >= lo, lane < lo + EXPERTS_PER_GROUP)
    e_log = jnp.where(in_grp, logits, ninf)
    v1 = jnp.max(e_log, axis=1, keepdims=True)
    i1 = jnp.min(jnp.where(e_log == v1, lane, LANE), axis=1, keepdims=True)
    e_log2 = jnp.where(lane == i1, ninf, e_log)
    v2 = jnp.max(e_log2, axis=1, keepdims=True)
    i2 = jnp.min(jnp.where(e_log2 == v2, lane, LANE), axis=1, keepdims=True)
    ex2 = jnp.exp(v2 - v1)
    gate1 = (1.0 / (1.0 + ex2)) * p_sel
    gate2 = (ex2 / (1.0 + ex2)) * p_sel
    e1 = i1 - N_GROUPS
    e2 = i2 - N_GROUPS

    oh1 = (lane == e1).astype(F32)
    oh2 = (lane == e2).astype(F32)
    both = oh1 + oh2
    r = lax.broadcasted_iota(jnp.int32, (MIX_TM, MIX_TM), 0)
    c = lax.broadcasted_iota(jnp.int32, (MIX_TM, MIX_TM), 1)
    strict = (r > c).astype(BF16)
    before = jnp.dot(strict, both.astype(BF16), preferred_element_type=F32) + run_ref[...]
    rank1 = jnp.sum(before * oh1, axis=1, keepdims=True)
    rank2 = jnp.sum(before * oh2, axis=1, keepdims=True)
    run_ref[...] = run_ref[...] + jnp.sum(both, axis=0, keepdims=True)
    cnt_ref[...] = jnp.broadcast_to(run_ref[...], cnt_ref.shape)

    out = jnp.zeros((MIX_TM, LANE), F32)
    for idx, val in enumerate((e1.astype(F32), e2.astype(F32), gate1, gate2, rank1, rank2)):
        out = jnp.where(lane == idx, val, out)
    route_ref[...] = out


def _mix_router(o_da, h_ml, proj, h, wout_bf, dng, mng, lng, lnb, wr_hi, wr_lo, br):
    const = lambda i: (0, 0)
    row = lambda i: (i, 0)
    return pl.pallas_call(
        _mix_router_kernel,
        grid=(ROWS // MIX_TM,),
        in_specs=[
            pl.BlockSpec((MIX_TM, D_DIFF), row),
            pl.BlockSpec((MIX_TM, D_MLSTM), row),
            pl.BlockSpec((MIX_TM, D_MLSTM), lambda i: (i, COL_MO // D_MLSTM)),
            pl.BlockSpec((MIX_TM, D_MODEL), row),
            pl.BlockSpec((D_MODEL, D_MODEL), const),
            pl.BlockSpec((1, DA_DV), const),
            pl.BlockSpec((1, D_MLSTM), const),
            pl.BlockSpec((1, D_MODEL), const),
            pl.BlockSpec((1, D_MODEL), const),
            pl.BlockSpec((D_MODEL, LANE), const),
            pl.BlockSpec((D_MODEL, LANE), const),
            pl.BlockSpec((1, LANE), const),
        ],
        out_specs=[
            pl.BlockSpec((MIX_TM, D_MODEL), row),
            pl.BlockSpec((MIX_TM, LANE), row),
            pl.BlockSpec((8, LANE), const),
        ],
        out_shape=[
            jax.ShapeDtypeStruct((ROWS, D_MODEL), F32),
            jax.ShapeDtypeStruct((ROWS, LANE), F32),
            jax.ShapeDtypeStruct((8, LANE), F32),
        ],
        scratch_shapes=[pltpu.VMEM((1, LANE), F32)],
        compiler_params=_cparams(("arbitrary",)),
        name="mix_router",
    )(o_da, h_ml, proj, h, wout_bf, dng, mng, lng, lnb, wr_hi, wr_lo, br)


def _dispatch_kernel(dest_ref, pend_ref, x_ref, xb_ref, zero_ref, sem, zsem):
    i = pl.program_id(0)
    base = i * (2 * ROW_TILE)

    def zero_block(start):
        return pltpu.make_async_copy(zero_ref, xb_ref.at[pl.ds(pl.multiple_of(start, MOE_BLOCK), MOE_BLOCK)], zsem)

    def zero_copy(e):
        return zero_block(pend_ref[e] - MOE_BLOCK)

    @pl.when(i == 0)
    def _():
        zero_ref[...] = jnp.zeros(zero_ref.shape, F32)
        first_free = pend_ref[N_EXPERTS - 1] // MOE_BLOCK
        for e in range(N_EXPERTS):
            prev = pend_ref[e - 1] if e else 0
            pl.when(pend_ref[e] > prev)(lambda e=e: zero_copy(e).start())
        lax.fori_loop(first_free, N_BLOCKS, lambda b, c: (zero_block(b * MOE_BLOCK).start(), c)[1], 0)
        for e in range(N_EXPERTS):
            prev = pend_ref[e - 1] if e else 0
            pl.when(pend_ref[e] > prev)(lambda e=e: zero_copy(e).wait())
        lax.fori_loop(first_free, N_BLOCKS, lambda b, c: (zero_block(b * MOE_BLOCK).wait(), c)[1], 0)

    def row_copy(r, k):
        return pltpu.make_async_copy(x_ref.at[pl.ds(r, 1)], xb_ref.at[pl.ds(dest_ref[base + 2 * r + k], 1)], sem)

    def issue(r, carry):
        row_copy(r, 0).start(priority=0)
        row_copy(r, 1).start(priority=1)
        return carry

    lax.fori_loop(0, ROW_TILE, issue, 0, unroll=DMA_UNROLL)
    for r in range(ROW_TILE):
        row_copy(r, 0).wait()
        row_copy(r, 1).wait()


def _dispatch(dest, pend, h2):
    grid_spec = pltpu.PrefetchScalarGridSpec(
        num_scalar_prefetch=2,
        grid=(N_ROW_TILES,),
        in_specs=[pl.BlockSpec((ROW_TILE, D_MODEL), lambda i, d, p: (i, 0))],
        out_specs=pl.BlockSpec(memory_space=pl.ANY),
        scratch_shapes=[pltpu.VMEM((MOE_BLOCK, D_MODEL), F32), pltpu.SemaphoreType.DMA, pltpu.SemaphoreType.DMA],
    )
    return pl.pallas_call(
        _dispatch_kernel,
        grid_spec=grid_spec,
        out_shape=jax.ShapeDtypeStruct((N_SLOTS, D_MODEL), F32),
        compiler_params=_cparams(("arbitrary",)),
        name="moe_dispatch",
    )(dest, pend, h2)


def _expert_kernel(be_ref, nx_ref, sl_ref, nu_ref, x_ref, wg_hbm, wu_hbm, wd_hbm, y_ref,
                   wg_f, wu_f, wd_f, wg_s, wu_s, wd_s, sem):
    b = pl.program_id(0)
    expert = be_ref[b]
    slot = sl_ref[b]
    used = b < nu_ref[0]
    first_of_run = jnp.logical_and(used, jnp.logical_or(b == 0, expert != be_ref[jnp.maximum(b - 1, 0)]))

    def weight_copies(e, s):
        return (pltpu.make_async_copy(wg_hbm.at[0, e], wg_f.at[s], sem.at[s]),
                pltpu.make_async_copy(wu_hbm.at[0, e], wu_f.at[s], sem.at[s]),
                pltpu.make_async_copy(wd_hbm.at[0, e], wd_f.at[s], sem.at[s]))

    @pl.when(jnp.logical_and(used, b == 0))
    def _():
        for c in weight_copies(expert, slot):
            c.start()

    @pl.when(first_of_run)
    def _():
        for c in weight_copies(expert, slot):
            c.wait()

        @pl.when(nx_ref[b] >= 0)
        def _():
            for c in weight_copies(nx_ref[b], 1 - slot):
                c.start()

        wg_s[...] = wg_f[slot].astype(BF16)
        wu_s[...] = wu_f[slot].astype(BF16)
        wd_s[...] = wd_f[slot].astype(BF16)

    @pl.when(used)
    def _():
        xb = x_ref[...].astype(BF16)
        gate = jnp.dot(xb, wg_s[...], preferred_element_type=F32)
        up = jnp.dot(xb, wu_s[...], preferred_element_type=F32)
        hh = (gate * jax.nn.sigmoid(gate)) * up
        y_ref[...] = jnp.dot(hh.astype(BF16), wd_s[...], preferred_element_type=F32)

    @pl.when(b >= nu_ref[0])
    def _():
        y_ref[...] = jnp.zeros(y_ref.shape, F32)


def _experts(block_expert, n_used, xb, w_gate, w_up, w_down):
    blocks = jnp.arange(N_BLOCKS, dtype=jnp.int32)
    starts = jnp.logical_or(blocks == 0, block_expert != jnp.roll(block_expert, 1))
    starts = jnp.logical_and(starts, blocks < n_used[0])
    run_slot = ((jnp.cumsum(starts.astype(jnp.int32)) - 1) & 1).astype(jnp.int32)
    start_idx = jnp.where(starts, blocks, N_BLOCKS)
    next_start = lax.cummin(jnp.concatenate([start_idx[1:], jnp.full((1,), N_BLOCKS, jnp.int32)]), reverse=True)
    next_expert = jnp.where(next_start < N_BLOCKS, block_expert[jnp.minimum(next_start, N_BLOCKS - 1)], -1)
    any_spec = pl.BlockSpec(memory_space=pl.ANY)
    grid_spec = pltpu.PrefetchScalarGridSpec(
        num_scalar_prefetch=4,
        grid=(N_BLOCKS,),
        in_specs=[
            pl.BlockSpec((MOE_BLOCK, D_MODEL), lambda b, be, nx, sl, nu: (jnp.minimum(b, nu[0] - 1), 0)),
            any_spec, any_spec, any_spec,
        ],
        out_specs=pl.BlockSpec((MOE_BLOCK, D_MODEL), lambda b, be, nx, sl, nu: (b, 0)),
        scratch_shapes=[
            pltpu.VMEM((2, D_MODEL, D_FF), F32),
            pltpu.VMEM((2, D_MODEL, D_FF), F32),
            pltpu.VMEM((2, D_FF, D_MODEL), F32),
            pltpu.VMEM((D_MODEL, D_FF), BF16),
            pltpu.VMEM((D_MODEL, D_FF), BF16),
            pltpu.VMEM((D_FF, D_MODEL), BF16),
            pltpu.SemaphoreType.DMA((2,)),
        ],
    )
    return pl.pallas_call(
        _expert_kernel,
        grid_spec=grid_spec,
        out_shape=jax.ShapeDtypeStruct((N_SLOTS, D_MODEL), F32),
        compiler_params=_cparams(("arbitrary",)),
        name="moe_experts",
    )(block_expert, next_expert.astype(jnp.int32), run_slot, n_used, xb, w_gate, w_up, w_down)


COMBINE_TILE0 = BLOCK // ROW_TILE
COMBINE_TILES = (SEQ + N_SAMPLE) // ROW_TILE
PROMPT_TILES = SEQ // ROW_TILE


def _combine_kernel(dest_ref, h2_ref, route_ref, g_ref, b_ref, yb_ref, yp_ref, ys_ref, buf, sem):
    i = pl.program_id(0)
    slot = lax.rem(i, 2)

    def row_copy(tile, slot_, r, k):
        src_row = dest_ref[(tile + COMBINE_TILE0) * (2 * ROW_TILE) + 2 * r + k]
        return pltpu.make_async_copy(yb_ref.at[pl.ds(src_row, 1)], buf.at[slot_, k, pl.ds(r, 1)], sem.at[slot_])

    def issue_tile(tile, slot_):
        def body(r, carry):
            row_copy(tile, slot_, r, 0).start(priority=0)
            row_copy(tile, slot_, r, 1).start(priority=1)
            return carry
        lax.fori_loop(0, ROW_TILE, body, 0, unroll=DMA_UNROLL)

    @pl.when(i == 0)
    def _():
        issue_tile(i, slot)

    for r in range(ROW_TILE):
        row_copy(i, slot, r, 0).wait()
        row_copy(i, slot, r, 1).wait()

    @pl.when(i + 1 < COMBINE_TILES)
    def _():
        issue_tile(i + 1, 1 - slot)

    route = route_ref[...]
    rows = buf[slot]
    y = route[:, 2:3] * rows[0] + route[:, 3:4] * rows[1]
    out = _layer_norm(ALPHA * h2_ref[...] + y, g_ref[...], b_ref[...])

    @pl.when(i < PROMPT_TILES)
    def _():
        yp_ref[...] = out

    @pl.when(i >= PROMPT_TILES)
    def _():
        ys_ref[...] = out


def _combine(dest, h2, route, g, b, yb):
    tile = lambda i, d: (i + COMBINE_TILE0, 0)
    grid_spec = pltpu.PrefetchScalarGridSpec(
        num_scalar_prefetch=1,
        grid=(COMBINE_TILES,),
        in_specs=[
            pl.BlockSpec((ROW_TILE, D_MODEL), tile),
            pl.BlockSpec((ROW_TILE, LANE), tile),
            pl.BlockSpec((1, D_MODEL), lambda i, d: (0, 0)),
            pl.BlockSpec((1, D_MODEL), lambda i, d: (0, 0)),
            pl.BlockSpec(memory_space=pl.ANY),
        ],
        out_specs=[
            pl.BlockSpec((ROW_TILE, D_MODEL), lambda i, d: (jnp.minimum(i, PROMPT_TILES - 1), 0)),
            pl.BlockSpec((ROW_TILE, D_MODEL), lambda i, d: (jnp.maximum(i - PROMPT_TILES, 0), 0)),
        ],
        scratch_shapes=[pltpu.VMEM((2, 2, ROW_TILE, D_MODEL), F32), pltpu.SemaphoreType.DMA((2,))],
    )
    return pl.pallas_call(
        _combine_kernel,
        grid_spec=grid_spec,
        out_shape=[jax.ShapeDtypeStruct((SEQ, D_MODEL), F32), jax.ShapeDtypeStruct((N_SAMPLE, D_MODEL), F32)],
        compiler_params=_cparams(("arbitrary",)),
        name="moe_combine",
    )(dest, h2, route, g, b, yb)


def _t5_bucket(rel):
    n = np.maximum(rel, 0)
    max_exact = N_BUCKETS // 2
    nf = np.maximum(n, max_exact).astype(np.float32)
    ratio = np.log(nf / np.float32(max_exact)) / np.float32(math.log(MAX_DISTANCE / max_exact))
    large = max_exact + (ratio * np.float32(N_BUCKETS - max_exact)).astype(np.int32)
    large = np.minimum(large, N_BUCKETS - 1)
    return np.where(n < max_exact, n, large)


def _rel_bias(rel_bias, rel):
    table = (rel_bias.astype(F32) - rel_bias.astype(F32)[N_BUCKETS - 1][None, :]) * LOG2E
    onehot = _t5_bucket(rel)[None, ..., None] == np.arange(N_BUCKETS)
    shaped = table.T.reshape((DA_HEADS,) + (1,) * rel.ndim + (N_BUCKETS,))
    return jnp.sum(jnp.where(onehot, shaped, 0.0), axis=-1)


def kernel(x_prompt, x_sample, cache_k, cache_v, state_C, state_n, state_m, page_table, meta_tokens, ln_in_g, ln_in_b, rel_bias, w_in, b_gates, lambda_q1, lambda_k1, lambda_q2, lambda_k2, diff_norm_g, mlstm_norm_g, w_out, ln_mix_g, ln_mix_b, w_router_g, b_router_g, w_router_e, b_router_e, w_gate, w_up, w_down, ln_ffn_g, ln_ffn_b):
    lam = (jnp.exp(jnp.sum(lambda_q1[0].astype(F32) * lambda_k1[0].astype(F32)))
           - jnp.exp(jnp.sum(lambda_q2[0].astype(F32) * lambda_k2[0].astype(F32))) + LAM_INIT).reshape(1)

    x = jnp.concatenate([
        jnp.zeros((META_PAD, D_MODEL), F32), meta_tokens.astype(F32), x_prompt.reshape(SEQ, D_MODEL),
        x_sample.reshape(N_SAMPLE, D_MODEL), jnp.zeros((ROWS - LP - N_SAMPLE, D_MODEL), F32)], axis=0)
    w_in_bf = w_in[0].astype(BF16)
    w_gate_cols = jnp.pad(w_in_bf[:, D_MAIN:], ((0, 0), (0, LANE - (D_IN - D_MAIN))))
    h, proj, gates, qlo, qhi, kbf, vbf = _ln_proj(
        x, ln_in_g.reshape(1, D_MODEL), ln_in_b.reshape(1, D_MODEL), w_in_bf[:, :D_MAIN], w_gate_cols)

    ar = np.arange(LANE)
    rel0 = ar[None, :] - ar[:, None]
    tiles = jnp.stack([_rel_bias(rel_bias, rel0), _rel_bias(rel_bias, rel0 + LANE)], axis=1)
    o_da = _prompt_attention(qlo, qhi, kbf, vbf, tiles, lam, jnp.zeros((ROWS, D_DIFF), F32))

    smp = slice(ROW_SAMPLE, ROW_SAMPLE + N_SAMPLE)
    q_s = proj[smp, :D_DIFF].reshape(DEC_BATCH, DEC_SEQ, DA_HEADS, 2, DA_DH) * Q_SCALE
    q5 = jnp.transpose(q_s, (0, 2, 3, 1, 4))
    eye_c = jnp.eye(2, dtype=F32)
    qt = (q5[:, :, :, :, None, :] * eye_c[None, None, :, None, :, None]).reshape(DEC_BATCH, SA_COLS, DA_DV).astype(BF16)
    zrows = jnp.zeros((DEC_BATCH, LANE - DEC_SEQ * DA_HEADS, DA_DV), F32)
    knew = jnp.concatenate([proj[smp, COL_K:COL_K + D_DIFF].reshape(DEC_BATCH, DEC_SEQ * DA_HEADS, DA_DV), zrows], axis=1)
    vnew = jnp.concatenate([proj[smp, COL_V:COL_V + D_DIFF].reshape(DEC_BATCH, DEC_SEQ * DA_HEADS, DA_DV), zrows], axis=1)
    qpos = np.arange(DEC_SEQ)
    tpos = np.arange(PAGE_SIZE)
    same_head = jnp.eye(DA_HEADS, dtype=bool)

    def score_bias(b_hqt, ok_qt):
        n_t = b_hqt.shape[-1]
        full = jnp.where(same_head[:, None, None, None, :] & ok_qt[None, None, :, :, None],
                         b_hqt[:, None, :, :, None], NEG)
        return jnp.broadcast_to(full, (DA_HEADS, 2, DEC_SEQ, n_t, DA_HEADS)).reshape(SA_COLS, n_t * DA_HEADS)

    all_ok = jnp.ones((DEC_SEQ, PAGE_SIZE), bool)
    b_last = _rel_bias(rel_bias, PAGE_SIZE + qpos[:, None] - tpos[None, :])
    bm = jnp.stack([score_bias(jnp.zeros_like(b_last), all_ok), score_bias(b_last, all_ok)])
    rel_new = qpos[:, None] - qpos[None, :]
    bmnew = jnp.concatenate([score_bias(_rel_bias(rel_bias, rel_new), rel_new >= 0),
                             jnp.full((SA_COLS, LANE - DEC_SEQ * DA_HEADS), NEG, F32)], axis=1)
    bmnew = jnp.stack([jnp.full((SA_COLS, LANE), NEG, F32), bmnew])
    o_da = _sample_attention(qt, knew, vnew, bm, bmnew, cache_k, cache_v, page_table.astype(jnp.int32), lam, o_da)

    gate_bias = jnp.pad(b_gates[0].astype(F32), (0, LANE - 2 * ML_HEADS)).reshape(1, LANE)
    h_ml, c_p, n_p, m_p = _mlstm(
        proj, gates, gate_bias, jnp.zeros((1, ML_HEADS, ML_DQK, ML_DV), F32),
        jnp.zeros((1, ML_HEADS, ML_DQK), F32), jnp.zeros((1, ML_HEADS, LANE), F32),
        jnp.zeros((ROWS, D_MLSTM), F32), n_seq=1, rows=BLOCK, carry=True, row_block0=0,
        n_steps=N_CHUNKS, pad_front=META_PAD)
    h_ml, c_s, n_s, m_s = _mlstm(
        proj, gates, gate_bias, state_C[0].astype(F32), state_n[0].astype(F32),
        jnp.broadcast_to(state_m[0].astype(F32)[:, :, None], (DEC_BATCH, ML_HEADS, LANE)), h_ml,
        n_seq=DEC_BATCH, rows=DEC_SEQ, carry=False, row_block0=ROW_SAMPLE // DEC_SEQ,
        n_steps=DEC_BATCH, pad_front=0)

    wr = jnp.pad(jnp.concatenate([w_router_g[0], w_router_e[0]], axis=1).astype(F32),
                 ((0, 0), (0, LANE - N_GROUPS - N_EXPERTS)))
    wr_hi = wr.astype(BF16)
    wr_lo = (wr - wr_hi.astype(F32)).astype(BF16)
    br = jnp.pad(jnp.concatenate([b_router_g[0], b_router_e[0]]).astype(F32),
                 (0, LANE - N_GROUPS - N_EXPERTS)).reshape(1, LANE)
    h2, route, counts = _mix_router(
        o_da, h_ml, proj, h, w_out[0].astype(BF16), diff_norm_g[0].reshape(1, DA_DV),
        mlstm_norm_g[0].reshape(1, D_MLSTM), ln_mix_g[0].reshape(1, D_MODEL), ln_mix_b[0].reshape(1, D_MODEL),
        wr_hi, wr_lo, br)

    cnt = counts[0, :N_EXPERTS].astype(jnp.int32)
    padded = (cnt + MOE_BLOCK - 1) // MOE_BLOCK * MOE_BLOCK
    pend = jnp.cumsum(padded)
    pstart = pend - padded
    e12 = route[:, 0:2].astype(jnp.int32)
    experts = jnp.arange(N_EXPERTS, dtype=jnp.int32)
    slot0 = jnp.sum(jnp.where(e12[:, :, None] == experts, pstart, 0), axis=-1)
    dest = (slot0 + route[:, 4:6].astype(jnp.int32)).reshape(N_ASSIGN)
    block_start = jnp.arange(N_BLOCKS, dtype=jnp.int32) * MOE_BLOCK
    block_expert = jnp.minimum(jnp.sum(block_start[:, None] >= pend[None, :], axis=1), N_EXPERTS - 1).astype(jnp.int32)
    n_used = (pend[-1] // MOE_BLOCK).astype(jnp.int32).reshape(1)

    xb = _dispatch(dest, pend.astype(jnp.int32), h2)
    yb = _experts(block_expert, n_used, xb, w_gate, w_up, w_down)
    y_p, y_s = _combine(dest, h2, route, ln_ffn_g[0].reshape(1, D_MODEL), ln_ffn_b[0].reshape(1, D_MODEL), yb)

    lq = N_META + SEQ
    y_prompt = y_p.reshape(1, SEQ, D_MODEL)
    y_sample = y_s.reshape(DEC_BATCH, DEC_SEQ, D_MODEL)
    k_prompt = proj[META_PAD:LP, COL_K:COL_K + D_DIFF].reshape(1, 1, lq, DA_HEADS, 2 * DA_DH)
    v_prompt = proj[META_PAD:LP, COL_V:COL_V + D_DIFF].reshape(1, 1, lq, DA_HEADS, DA_DV)
    k_sample = proj[smp, COL_K:COL_K + D_DIFF].reshape(1, DEC_BATCH, DEC_SEQ, DA_HEADS, 2 * DA_DH)
    v_sample = proj[smp, COL_V:COL_V + D_DIFF].reshape(1, DEC_BATCH, DEC_SEQ, DA_HEADS, DA_DV)
    return (y_prompt, y_sample, k_prompt, v_prompt,
            c_p[None], n_p[None], m_p[None, :, :, 0],
            k_sample, v_sample, c_s[None], n_s[None], m_s[None, :, :, 0])
```

```python
import functools
import math

import numpy as np
import jax
import jax.numpy as jnp
from jax import lax
from jax.experimental import pallas as pl
from jax.experimental.pallas import tpu as pltpu

F32 = jnp.float32
BF16 = jnp.bfloat16

D_MODEL = 2048
SEQ = 8192
DEC_BATCH = 32
DEC_SEQ = 8
PAGE_SIZE = 128
N_PAGES = 128
N_META = 16
BLOCK = 128
META_PAD = 112
DA_HEADS = 8
DA_DH = 64
DA_DV = 128
D_DIFF = 1024
ML_HEADS = 4
ML_DQK = 128
ML_DV = 256
D_MLSTM = 1024
D_IN = 6152
N_BUCKETS = 32
MAX_DISTANCE = 128
N_GROUPS = 4
EXPERTS_PER_GROUP = 8
N_EXPERTS = 32
D_FF = 512
ALPHA = 2.0 ** 0.25
LN_EPS = 1e-5
NEG = -1e30
LAM_INIT = 0.8 - 0.6 * math.exp(-0.0)
LOG2E = math.log2(math.e)
Q_SCALE = DA_DH ** -0.5 * LOG2E

LANE = 128
LP = META_PAD + N_META + SEQ
N_CHUNKS = LP // BLOCK
N_SAMPLE = DEC_BATCH * DEC_SEQ
ROW_SAMPLE = LP
ROWS = LP + N_SAMPLE + BLOCK
D_MAIN = 6144
COL_K = 1024
COL_V = 2048
COL_MQ = 3072
COL_MK = 3584
COL_MV = 4096
COL_MO = 5120

PROJ_TM = 512
PROJ_TN = 1024
ATT_T = 640
ATT_NB = LP // ATT_T
ATT_HEADS = 4
ATT_ONES = 16
PAGES_PER_STEP = 8
PAGE_LOOKAHEAD = 2
PAGE_SLOTS = PAGE_LOOKAHEAD + 1
SA_STEPS = N_PAGES // PAGES_PER_STEP
PAGE_ROWS = PAGE_SIZE * DA_HEADS
SA_COLS = DA_HEADS * 2 * DEC_SEQ
MOE_BLOCK = 256
N_ASSIGN = 2 * ROWS
N_BLOCKS = N_ASSIGN // MOE_BLOCK + N_EXPERTS
N_SLOTS = N_BLOCKS * MOE_BLOCK
ROW_TILE = 128
N_ROW_TILES = ROWS // ROW_TILE
MIX_TM = 512
DMA_UNROLL = 8

VMEM_LIMIT = 56 * 1024 * 1024


def _cparams(sem, vmem=VMEM_LIMIT):
    return pltpu.CompilerParams(dimension_semantics=sem, vmem_limit_bytes=vmem)


def _layer_norm(x, g, b):
    mu = jnp.mean(x, axis=-1, keepdims=True)
    xc = x - mu
    var = jnp.mean(xc * xc, axis=-1, keepdims=True)
    return xc * lax.rsqrt(var + LN_EPS) * g + b


_NT = (((1,), (1,)), ((), ()))


def _ln_proj_kernel(x_ref, g_ref, b_ref, w_ref, wg_ref, h_ref, proj_ref, gate_ref,
                    qlo_ref, qhi_ref, kb_ref, vb_ref, xn_ref):
    j = pl.program_id(1)

    @pl.when(j == 0)
    def _():
        hn = _layer_norm(x_ref[...], g_ref[...], b_ref[...])
        h_ref[...] = hn
        xn = hn.astype(BF16)
        xn_ref[...] = xn
        gate_ref[...] = jnp.dot(xn, wg_ref[...], preferred_element_type=F32)

    acc = jnp.dot(xn_ref[...], w_ref[...], preferred_element_type=F32)
    proj_ref[...] = acc

    @pl.when(j == 0)
    def _():
        lane = lax.broadcasted_iota(jnp.int32, acc.shape, 1)
        first = (lane & (DA_DV - 1)) < DA_DH
        qs = acc * Q_SCALE
        qlo_ref[...] = jnp.where(first, qs, 0.0).astype(BF16)
        qhi_ref[...] = jnp.where(first, 0.0, qs).astype(BF16)

    @pl.when(j == 1)
    def _():
        kb_ref[...] = acc.astype(BF16)

    @pl.when(j == 2)
    def _():
        vb_ref[...] = acc.T.astype(BF16)


def _ln_proj(x, g, b, w_bf, wg_bf):
    row = lambda i, j: (i, 0)
    bf_shape = jax.ShapeDtypeStruct((ROWS, D_DIFF), BF16)
    return pl.pallas_call(
        _ln_proj_kernel,
        grid=(ROWS // PROJ_TM, D_MAIN // PROJ_TN),
        in_specs=[
            pl.BlockSpec((PROJ_TM, D_MODEL), row),
            pl.BlockSpec((1, D_MODEL), lambda i, j: (0, 0)),
            pl.BlockSpec((1, D_MODEL), lambda i, j: (0, 0)),
            pl.BlockSpec((D_MODEL, PROJ_TN), lambda i, j: (0, j)),
            pl.BlockSpec((D_MODEL, LANE), lambda i, j: (0, 0)),
        ],
        out_specs=[
            pl.BlockSpec((PROJ_TM, D_MODEL), row),
            pl.BlockSpec((PROJ_TM, PROJ_TN), lambda i, j: (i, j)),
            pl.BlockSpec((PROJ_TM, LANE), row),
            pl.BlockSpec((PROJ_TM, D_DIFF), row),
            pl.BlockSpec((PROJ_TM, D_DIFF), row),
            pl.BlockSpec((PROJ_TM, D_DIFF), row),
            pl.BlockSpec((D_DIFF, PROJ_TM), lambda i, j: (0, i)),
        ],
        out_shape=[
            jax.ShapeDtypeStruct((ROWS, D_MODEL), F32),
            jax.ShapeDtypeStruct((ROWS, D_MAIN), F32),
            jax.ShapeDtypeStruct((ROWS, LANE), F32),
            bf_shape, bf_shape, bf_shape,
            jax.ShapeDtypeStruct((D_DIFF, ROWS), BF16),
        ],
        scratch_shapes=[pltpu.VMEM((PROJ_TM, D_MODEL), BF16)],
        compiler_params=_cparams(("arbitrary", "arbitrary")),
        name="ln_proj",
    )(x, g, b, w_bf, wg_bf)


def _attn_kernel(qi_ref, kj_ref, lam_ref, qlo_ref, qhi_ref, k_ref, vt_ref, t_ref, o_init_ref, o_ref,
                 bd_ref, bs_ref, *state_refs):
    m_refs = state_refs[:2 * ATT_HEADS]
    acc_refs = state_refs[2 * ATT_HEADS:]
    del o_init_ref
    t = pl.program_id(1)
    i = qi_ref[t]
    j = kj_ref[t]
    nsub = ATT_T // LANE

    @pl.when(t == 0)
    def _():
        z = jnp.zeros((LANE, LANE), F32)
        for a in range(ATT_HEADS):
            t0 = t_ref[a, 0]
            t1 = t_ref[a, 1]
            for r in range(nsub):
                bd_ref[a, r * LANE:(r + 1) * LANE, :] = jnp.concatenate(
                    [t0 if c == r else (t1 if c == r + 1 else z) for c in range(nsub)], axis=1)
                bs_ref[a, r * LANE:(r + 1) * LANE, :] = jnp.concatenate(
                    [t1 if (r == nsub - 1 and c == 0) else z for c in range(nsub)], axis=1)

    @pl.when(j == 0)
    def _():
        for m_ref, acc_ref in zip(m_refs, acc_refs):
            m_ref[...] = jnp.full(m_ref.shape, -jnp.inf, F32)
            acc_ref[...] = jnp.zeros(acc_ref.shape, F32)

    ones_rows = jnp.ones((ATT_ONES, ATT_T), BF16)

    def update(bias_ref, causal, padmask):
        if causal or padmask:
            r = lax.broadcasted_iota(jnp.int32, (ATT_T, ATT_T), 0)
            c = lax.broadcasted_iota(jnp.int32, (ATT_T, ATT_T), 1)
            valid = (r + j * ATT_T) >= META_PAD
            if causal:
                valid = jnp.logical_and(valid, r <= c)
        def score(idx):
            a, comp = divmod(idx, 2)
            sl = slice(a * DA_DV, (a + 1) * DA_DV)
            q_ref = (qlo_ref, qhi_ref)[comp]
            s = lax.dot_general(k_ref[:, sl], q_ref[:, sl], _NT, preferred_element_type=F32)
            if bias_ref is not None:
                s = s + bias_ref[a]
            if causal or padmask:
                s = jnp.where(valid, s, NEG)
            return s

        def accumulate(idx, s, m_prev, m_new):
            a = idx // 2
            vt = jnp.concatenate([vt_ref[a * DA_DV:(a + 1) * DA_DV, :], ones_rows], axis=0)
            acc_ref = acc_refs[idx]
            alpha = jnp.exp2(m_prev - m_new)
            p = jnp.exp2(s - m_new).astype(BF16)
            acc_ref[...] = alpha * acc_ref[...] + jnp.dot(vt, p, preferred_element_type=F32)

        n_chain = 2 * ATT_HEADS
        scores = [score(idx) for idx in range(n_chain)]
        m_prevs = [m_refs[idx][...] for idx in range(n_chain)]
        m_news = [jnp.maximum(mp, jnp.max(s, axis=0, keepdims=True)) for mp, s in zip(m_prevs, scores)]
        for idx in range(n_chain):
            m_refs[idx][...] = m_news[idx]
        for idx in range(n_chain):
            accumulate(idx, scores[idx], m_prevs[idx], m_news[idx])

    @pl.when(j == i)
    def _():
        update(bd_ref, True, True)
        for a in range(ATT_HEADS):
            n1 = acc_refs[2 * a][...]
            n2 = acc_refs[2 * a + 1][...]
            ot = (n1[:DA_DV] / n1[DA_DV:DA_DV + 1] - lam_ref[0] * (n2[:DA_DV] / n2[DA_DV:DA_DV + 1]))
            o_ref[:, a * DA_DV:(a + 1) * DA_DV] = ot.T

    @pl.when(j == i - 1)
    def _():
        update(bs_ref, False, True)

    @pl.when(jnp.logical_and(j == 0, i >= 2))
    def _():
        update(None, False, True)

    @pl.when(jnp.logical_and(j >= 1, j <= i - 2))
    def _():
        update(None, False, False)


def _prompt_attention(qlo, qhi, kb, vt, tiles, lam, o_init):
    pairs = [(i, j) for i in range(ATT_NB) for j in range(i + 1)]
    qi = jnp.asarray([p[0] for p in pairs], jnp.int32)
    kj = jnp.asarray([p[1] for p in pairs], jnp.int32)
    width = ATT_HEADS * DA_DV
    qmap = lambda h, t, qi, kj: (qi[t], h)
    kmap = lambda h, t, qi, kj: (kj[t], h)
    grid_spec = pltpu.PrefetchScalarGridSpec(
        num_scalar_prefetch=2,
        grid=(DA_HEADS // ATT_HEADS, len(pairs)),
        in_specs=[
            pl.BlockSpec(memory_space=pltpu.SMEM),
            pl.BlockSpec((ATT_T, width), qmap),
            pl.BlockSpec((ATT_T, width), qmap),
            pl.BlockSpec((ATT_T, width), kmap),
            pl.BlockSpec((width, ATT_T), lambda h, t, qi, kj: (h, kj[t])),
            pl.BlockSpec((ATT_HEADS, 2, LANE, LANE), lambda h, t, qi, kj: (h, 0, 0, 0)),
            pl.BlockSpec(memory_space=pl.ANY),
        ],
        out_specs=pl.BlockSpec((ATT_T, width), qmap),
        scratch_shapes=[
            pltpu.VMEM((ATT_HEADS, ATT_T, ATT_T), F32),
            pltpu.VMEM((ATT_HEADS, ATT_T, ATT_T), F32),
        ] + [pltpu.VMEM((1, ATT_T), F32)] * (2 * ATT_HEADS) + [
            pltpu.VMEM((DA_DV + ATT_ONES, ATT_T), F32)] * (2 * ATT_HEADS) + [
        ],
    )
    return pl.pallas_call(
        _attn_kernel,
        grid_spec=grid_spec,
        out_shape=jax.ShapeDtypeStruct((ROWS, D_DIFF), F32),
        input_output_aliases={8: 0},
        compiler_params=_cparams(("arbitrary", "arbitrary")),
        name="prompt_attn",
    )(qi, kj, lam, qlo, qhi, kb, vt, tiles, o_init)


def _sattn_kernel(pt_ref, lam_ref, qt_ref, knew_ref, vnew_ref, bm_ref, bmnew_ref, ck_ref, cv_ref, o_init_ref,
                  o_ref, m_ref, l_ref, acc_ref, kbuf, vbuf, sem):
    del o_init_ref
    b = pl.program_id(0)
    g = pl.program_id(1)
    step = b * SA_STEPS + g
    n_steps = DEC_BATCH * SA_STEPS
    last = g == SA_STEPS - 1

    def page_copies(step_, slot_):
        seq = step_ // SA_STEPS
        first = (step_ - seq * SA_STEPS) * PAGES_PER_STEP
        copies = []
        for r in range(PAGES_PER_STEP):
            page = pt_ref[seq, first + r]
            copies.append(pltpu.make_async_copy(ck_ref.at[page], kbuf.at[slot_, r], sem.at[slot_]))
            copies.append(pltpu.make_async_copy(cv_ref.at[page], vbuf.at[slot_, r], sem.at[slot_]))
        return copies

    def fetch(step_):
        for c in page_copies(step_, lax.rem(step_, PAGE_SLOTS)):
            c.start()

    @pl.when(step == 0)
    def _():
        for ahead in range(PAGE_LOOKAHEAD):
            fetch(step + ahead)

    @pl.when(step + PAGE_LOOKAHEAD < n_steps)
    def _():
        fetch(step + PAGE_LOOKAHEAD)

    slot = lax.rem(step, PAGE_SLOTS)
    for c in page_copies(step, slot):
        c.wait()

    @pl.when(g == 0)
    def _():
        m_ref[...] = jnp.full(m_ref.shape, -jnp.inf, F32)
        l_ref[...] = jnp.zeros(l_ref.shape, F32)
        acc_ref[...] = jnp.zeros(acc_ref.shape, F32)

    qt = qt_ref[...]
    is_last = last.astype(jnp.int32)

    k_blocks = [kbuf[slot, r] for r in range(PAGES_PER_STEP)] + [knew_ref[...]]
    v_blocks = [vbuf[slot, r] for r in range(PAGES_PER_STEP)] + [vnew_ref[...]]
    biases = [bm_ref[0]] * (PAGES_PER_STEP - 1) + [bm_ref[is_last], bmnew_ref[is_last]]
    scores = [lax.dot_general(qt, kr.astype(BF16), _NT, preferred_element_type=F32) + b
              for kr, b in zip(k_blocks, biases)]
    m_prev = m_ref[...]
    m_new = m_prev
    for s in scores:
        m_new = jnp.maximum(m_new, jnp.max(s, axis=1, keepdims=True))
    alpha = jnp.exp2(m_prev - m_new)
    l_new = alpha * l_ref[...]
    acc_new = alpha * acc_ref[...]
    for s, vr in zip(scores, v_blocks):
        p = jnp.exp2(s - m_new)
        l_new = l_new + jnp.sum(p, axis=1, keepdims=True)
        acc_new = acc_new + jnp.dot(p.astype(BF16), vr.astype(BF16), preferred_element_type=F32)
    m_ref[...] = m_new
    l_ref[...] = l_new
    acc_ref[...] = acc_new

    @pl.when(last)
    def _():
        oc = acc_new / l_new
        lam = lam_ref[0]
        rows_per_head = 2 * DEC_SEQ
        for h in range(DA_HEADS):
            r0 = h * rows_per_head
            o_ref[:, h * DA_DV:(h + 1) * DA_DV] = oc[r0:r0 + DEC_SEQ] - lam * oc[r0 + DEC_SEQ:r0 + 2 * DEC_SEQ]


def _sample_attention(qt, knew, vnew, bm, bmnew, cache_k, cache_v, page_table, lam, o_init):
    per_seq = lambda b, g, pt: (b, 0, 0)
    any_spec = pl.BlockSpec(memory_space=pl.ANY)
    grid_spec = pltpu.PrefetchScalarGridSpec(
        num_scalar_prefetch=1,
        grid=(DEC_BATCH, SA_STEPS),
        in_specs=[
            pl.BlockSpec(memory_space=pltpu.SMEM),
            pl.BlockSpec((None, SA_COLS, DA_DV), per_seq),
            pl.BlockSpec((None, LANE, DA_DV), per_seq),
            pl.BlockSpec((None, LANE, DA_DV), per_seq),
            pl.BlockSpec((2, SA_COLS, PAGE_ROWS), lambda b, g, pt: (0, 0, 0)),
            pl.BlockSpec((2, SA_COLS, LANE), lambda b, g, pt: (0, 0, 0)),
            any_spec, any_spec, any_spec,
        ],
        out_specs=pl.BlockSpec((DEC_SEQ, D_DIFF), lambda b, g, pt: (ROW_SAMPLE // DEC_SEQ + b, 0)),
        scratch_shapes=[
            pltpu.VMEM((SA_COLS, 1), F32),
            pltpu.VMEM((SA_COLS, 1), F32),
            pltpu.VMEM((SA_COLS, DA_DV), F32),
            pltpu.VMEM((PAGE_SLOTS, PAGES_PER_STEP, PAGE_ROWS, DA_DV), F32),
            pltpu.VMEM((PAGE_SLOTS, PAGES_PER_STEP, PAGE_ROWS, DA_DV), F32),
            pltpu.SemaphoreType.DMA((PAGE_SLOTS,)),
        ],
    )
    ck = cache_k.reshape(cache_k.shape[1], PAGE_ROWS, DA_DV)
    cv = cache_v.reshape(cache_v.shape[1], PAGE_ROWS, DA_DV)
    return pl.pallas_call(
        _sattn_kernel,
        grid_spec=grid_spec,
        out_shape=jax.ShapeDtypeStruct((ROWS, D_DIFF), F32),
        input_output_aliases={9: 0},
        compiler_params=_cparams(("arbitrary", "arbitrary")),
        name="sample_attn",
    )(page_table, lam, qt, knew, vnew, bm, bmnew, ck, cv, o_init)


def _mlstm_kernel(q_ref, k_ref, v_ref, g_ref, gb_ref, c0_ref, n0_ref, m0_ref, h_init_ref,
                  h_ref, c_ref, n_ref, m_ref, cs, ns, ms, *, rows, carry, pad_front):
    del h_init_ref
    step = pl.program_id(0)
    L = BLOCK

    def load_state():
        cs[...] = c0_ref[...]
        ns[...] = n0_ref[...]
        ms[...] = m0_ref[...]

    if carry:
        pl.when(step == 0)(load_state)
    else:
        load_state()

    def pad_rows(x):
        if rows == L:
            return x
        return jnp.concatenate([x, jnp.zeros((L - rows, x.shape[1]), x.dtype)], axis=0)

    row = lax.broadcasted_iota(jnp.int32, (L, L), 0)
    col = lax.broadcasted_iota(jnp.int32, (L, L), 1)
    not_token = row >= rows
    if pad_front:
        not_token = jnp.logical_or(not_token, row + step * L < pad_front)
    pre = pad_rows(g_ref[...]) + gb_ref[...]
    logi = jnp.where(not_token, NEG, pre)
    logf = jnp.where(not_token, 0.0, jnp.minimum(pre, 0.0) - jnp.log1p(jnp.exp(-jnp.abs(pre))))
    tril = (row >= col).astype(F32)
    bcum = jnp.dot(tril, logf, preferred_element_type=F32, precision=lax.Precision.HIGHEST)
    bcum_t = bcum.T
    logi_t = logi.T

    qa = pad_rows(q_ref[...]) * (ML_DQK ** -0.5)
    ka = pad_rows(k_ref[...])
    va = pad_rows(v_ref[...])
    heads = range(ML_HEADS)
    qh = [qa[:, h * ML_DQK:(h + 1) * ML_DQK] for h in heads]
    kh = [ka[:, h * ML_DQK:(h + 1) * ML_DQK] for h in heads]
    qb = [x.astype(BF16) for x in qh]
    vb = [va[:, h * ML_DV:(h + 1) * ML_DV].astype(BF16) for h in heads]
    c_old = [cs[h] for h in heads]
    n_old = [ns[h:h + 1, :] for h in heads]
    m_prev = [ms[h:h + 1, 0:1] for h in heads]
    qk = [lax.dot_general(qb[h], kh[h].astype(BF16), _NT, preferred_element_type=F32) for h in heads]
    qc = [jnp.dot(qb[h], c_old[h].astype(BF16), preferred_element_type=F32) for h in heads]
    bc = [bcum[:, ML_HEADS + h:ML_HEADS + h + 1] for h in heads]
    ic = [logi[:, h:h + 1] for h in heads]
    d = [jnp.where(row >= col, (bc[h] - bcum_t[ML_HEADS + h:ML_HEADS + h + 1, :]) + logi_t[h:h + 1, :], -jnp.inf)
         for h in heads]
    m_inter = [m_prev[h] + bc[h] for h in heads]
    m_t = [jnp.maximum(m_inter[h], jnp.max(d[h], axis=1, keepdims=True)) for h in heads]
    s = [jnp.exp(d[h] - m_t[h]) * qk[h] for h in heads]
    inter = [jnp.exp(m_inter[h] - m_t[h]) for h in heads]
    num = [jnp.dot(s[h].astype(BF16), vb[h], preferred_element_type=F32) + inter[h] * qc[h] for h in heads]
    den = [jnp.sum(s[h], axis=1, keepdims=True) + inter[h] * jnp.sum(qh[h] * n_old[h], axis=1, keepdims=True)
           for h in heads]
    for h in heads:
        hh = num[h] / jnp.maximum(jnp.abs(den[h]), jnp.exp(-m_t[h]))
        h_ref[:, h * ML_DV:(h + 1) * ML_DV] = hh[:rows]
    m_new = [m_t[h][L - 1:L, :] for h in heads]
    kw = [kh[h] * jnp.exp(bc[h][L - 1:L, :] - bc[h] + ic[h] - m_new[h]) for h in heads]
    decay = [jnp.exp(m_inter[h][L - 1:L, :] - m_new[h]) for h in heads]
    for h in heads:
        cs[h] = decay[h] * c_old[h] + jnp.dot(kw[h].T.astype(BF16), vb[h], preferred_element_type=F32)
        ns[h:h + 1, :] = decay[h] * n_old[h] + jnp.sum(kw[h], axis=0, keepdims=True)
        ms[h:h + 1, :] = jnp.broadcast_to(m_new[h], (1, LANE))

    c_ref[...] = cs[...]
    n_ref[...] = ns[...]
    m_ref[...] = ms[...]


def _mlstm(proj, gates, gate_bias, c0, n0, m0, h_init, *, n_seq, rows, carry, row_block0, n_steps, pad_front):
    seq = (lambda s: 0) if carry else (lambda s: s)
    kernel = functools.partial(_mlstm_kernel, rows=rows, carry=carry, pad_front=pad_front)
    dqk = ML_HEADS * ML_DQK
    return pl.pallas_call(
        kernel,
        grid=(n_steps,),
        in_specs=[
            pl.BlockSpec((rows, dqk), lambda s: (row_block0 + s, COL_MQ // dqk)),
            pl.BlockSpec((rows, dqk), lambda s: (row_block0 + s, COL_MK // dqk)),
            pl.BlockSpec((rows, D_MLSTM), lambda s: (row_block0 + s, COL_MV // D_MLSTM)),
            pl.BlockSpec((rows, LANE), lambda s: (row_block0 + s, 0)),
            pl.BlockSpec((1, LANE), lambda s: (0, 0)),
            pl.BlockSpec((None, ML_HEADS, ML_DQK, ML_DV), lambda s: (seq(s), 0, 0, 0)),
            pl.BlockSpec((None, ML_HEADS, ML_DQK), lambda s: (seq(s), 0, 0)),
            pl.BlockSpec((None, ML_HEADS, LANE), lambda s: (seq(s), 0, 0)),
            pl.BlockSpec(memory_space=pl.ANY),
        ],
        out_specs=[
            pl.BlockSpec((rows, D_MLSTM), lambda s: (row_block0 + s, 0)),
            pl.BlockSpec((None, ML_HEADS, ML_DQK, ML_DV), lambda s: (seq(s), 0, 0, 0)),
            pl.BlockSpec((None, ML_HEADS, ML_DQK), lambda s: (seq(s), 0, 0)),
            pl.BlockSpec((None, ML_HEADS, LANE), lambda s: (seq(s), 0, 0)),
        ],
        out_shape=[
            jax.ShapeDtypeStruct((ROWS, D_MLSTM), F32),
            jax.ShapeDtypeStruct((n_seq, ML_HEADS, ML_DQK, ML_DV), F32),
            jax.ShapeDtypeStruct((n_seq, ML_HEADS, ML_DQK), F32),
            jax.ShapeDtypeStruct((n_seq, ML_HEADS, LANE), F32),
        ],
        scratch_shapes=[
            pltpu.VMEM((ML_HEADS, ML_DQK, ML_DV), F32),
            pltpu.VMEM((ML_HEADS, ML_DQK), F32),
            pltpu.VMEM((ML_HEADS, LANE), F32),
        ],
        input_output_aliases={8: 0},
        compiler_params=_cparams(("arbitrary",)),
        name="mlstm_prompt" if carry else "mlstm_sample",
    )(proj, proj, proj, gates, gate_bias, c0, n0, m0, h_init)


def _mix_router_kernel(o_ref, hm_ref, mo_ref, h_ref, wout_ref, dng_ref, mng_ref,
                       lng_ref, lnb_ref, wrh_ref, wrl_ref, br_ref, h2_ref, route_ref, cnt_ref, run_ref):
    @pl.when(pl.program_id(0) == 0)
    def _():
        run_ref[...] = jnp.zeros(run_ref.shape, F32)

    o_da = o_ref[...]
    h_ml = hm_ref[...]
    parts = []
    for hd in range(DA_HEADS):
        seg = o_da[:, hd * DA_DV:(hd + 1) * DA_DV]
        ms = jnp.mean(seg * seg, axis=1, keepdims=True)
        parts.append((seg * lax.rsqrt(ms + LN_EPS) * dng_ref[...] * (1.0 - LAM_INIT)).astype(BF16))
    mo = mo_ref[...]
    for hd in range(ML_HEADS):
        sl = slice(hd * ML_DV, (hd + 1) * ML_DV)
        seg = h_ml[:, sl]
        ms = jnp.mean(seg * seg, axis=1, keepdims=True)
        parts.append((jax.nn.sigmoid(mo[:, sl]) * (seg * lax.rsqrt(ms + LN_EPS) * mng_ref[:, sl])).astype(BF16))
    mix = jnp.dot(jnp.concatenate(parts, axis=1), wout_ref[...], preferred_element_type=F32)
    h2 = _layer_norm(ALPHA * h_ref[...] + mix, lng_ref[...], lnb_ref[...])
    h2_ref[...] = h2

    h2_hi = h2.astype(BF16)
    h2_lo = (h2 - h2_hi.astype(F32)).astype(BF16)
    logits = (jnp.dot(h2_hi, wrh_ref[...], preferred_element_type=F32)
              + (jnp.dot(h2_lo, wrh_ref[...], preferred_element_type=F32)
                 + jnp.dot(h2_hi, wrl_ref[...], preferred_element_type=F32))) + br_ref[...]
    lane = lax.broadcasted_iota(jnp.int32, (MIX_TM, LANE), 1)
    ninf = -jnp.inf
    g_log = jnp.where(lane < N_GROUPS, logits, ninf)
    g_max = jnp.max(g_log, axis=1, keepdims=True)
    g_sel = jnp.min(jnp.where(g_log == g_max, lane, LANE), axis=1, keepdims=True)
    p_sel = 1.0 / jnp.sum(jnp.exp(g_log - g_max), axis=1, keepdims=True)
    lo = N_GROUPS + g_sel * EXPERTS_PER_GROUP
    in_grp = jnp.logical_and(lane >= lo, lane < lo + EXPERTS_PER_GROUP)
    e_log = jnp.where(in_grp, logits, ninf)
    v1 = jnp.max(e_log, axis=1, keepdims=True)
    i1 = jnp.min(jnp.where(e_log == v1, lane, LANE), axis=1, keepdims=True)
    e_log2 = jnp.where(lane == i1, ninf, e_log)
    v2 = jnp.max(e_log2, axis=1, keepdims=True)
    i2 = jnp.min(jnp.where(e_log2 == v2, lane, LANE), axis=1, keepdims=True)
    ex2 = jnp.exp(v2 - v1)
    gate1 = (1.0 / (1.0 + ex2)) * p_sel
    gate2 = (ex2 / (1.0 + ex2)) * p_sel
    e1 = i1 - N_GROUPS
    e2 = i2 - N_GROUPS

    oh1 = (lane == e1).astype(F32)
    oh2 = (lane == e2).astype(F32)
    both = oh1 + oh2
    r = lax.broadcasted_iota(jnp.int32, (MIX_TM, MIX_TM), 0)
    c = lax.broadcasted_iota(jnp.int32, (MIX_TM, MIX_TM), 1)
    strict = (r > c).astype(BF16)
    before = jnp.dot(strict, both.astype(BF16), preferred_element_type=F32) + run_ref[...]
    rank1 = jnp.sum(before * oh1, axis=1, keepdims=True)
    rank2 = jnp.sum(before * oh2, axis=1, keepdims=True)
    run_ref[...] = run_ref[...] + jnp.sum(both, axis=0, keepdims=True)
    cnt_ref[...] = jnp.broadcast_to(run_ref[...], cnt_ref.shape)

    out = jnp.zeros((MIX_TM, LANE), F32)
    for idx, val in enumerate((e1.astype(F32), e2.astype(F32), gate1, gate2, rank1, rank2)):
        out = jnp.where(lane == idx, val, out)
    route_ref[...] = out


def _mix_router(o_da, h_ml, proj, h, wout_bf, dng, mng, lng, lnb, wr_hi, wr_lo, br):
    const = lambda i: (0, 0)
    row = lambda i: (i, 0)
    return pl.pallas_call(
        _mix_router_kernel,
        grid=(ROWS // MIX_TM,),
        in_specs=[
            pl.BlockSpec((MIX_TM, D_DIFF), row),
            pl.BlockSpec((MIX_TM, D_MLSTM), row),
            pl.BlockSpec((MIX_TM, D_MLSTM), lambda i: (i, COL_MO // D_MLSTM)),
            pl.BlockSpec((MIX_TM, D_MODEL), row),
            pl.BlockSpec((D_MODEL, D_MODEL), const),
            pl.BlockSpec((1, DA_DV), const),
            pl.BlockSpec((1, D_MLSTM), const),
            pl.BlockSpec((1, D_MODEL), const),
            pl.BlockSpec((1, D_MODEL), const),
            pl.BlockSpec((D_MODEL, LANE), const),
            pl.BlockSpec((D_MODEL, LANE), const),
            pl.BlockSpec((1, LANE), const),
        ],
        out_specs=[
            pl.BlockSpec((MIX_TM, D_MODEL), row),
            pl.BlockSpec((MIX_TM, LANE), row),
            pl.BlockSpec((8, LANE), const),
        ],
        out_shape=[
            jax.ShapeDtypeStruct((ROWS, D_MODEL), F32),
            jax.ShapeDtypeStruct((ROWS, LANE), F32),
            jax.ShapeDtypeStruct((8, LANE), F32),
        ],
        scratch_shapes=[pltpu.VMEM((1, LANE), F32)],
        compiler_params=_cparams(("arbitrary",)),
        name="mix_router",
    )(o_da, h_ml, proj, h, wout_bf, dng, mng, lng, lnb, wr_hi, wr_lo, br)


def _dispatch_kernel(dest_ref, pend_ref, x_ref, xb_ref, zero_ref, sem, zsem):
    i = pl.program_id(0)
    base = i * (2 * ROW_TILE)

    def zero_block(start):
        return pltpu.make_async_copy(zero_ref, xb_ref.at[pl.ds(pl.multiple_of(start, MOE_BLOCK), MOE_BLOCK)], zsem)

    def zero_copy(e):
        return zero_block(pend_ref[e] - MOE_BLOCK)

    @pl.when(i == 0)
    def _():
        zero_ref[...] = jnp.zeros(zero_ref.shape, F32)
        first_free = pend_ref[N_EXPERTS - 1] // MOE_BLOCK
        for e in range(N_EXPERTS):
            prev = pend_ref[e - 1] if e else 0
            pl.when(pend_ref[e] > prev)(lambda e=e: zero_copy(e).start())
        lax.fori_loop(first_free, N_BLOCKS, lambda b, c: (zero_block(b * MOE_BLOCK).start(), c)[1], 0)
        for e in range(N_EXPERTS):
            prev = pend_ref[e - 1] if e else 0
            pl.when(pend_ref[e] > prev)(lambda e=e: zero_copy(e).wait())
        lax.fori_loop(first_free, N_BLOCKS, lambda b, c: (zero_block(b * MOE_BLOCK).wait(), c)[1], 0)

    def row_copy(r, k):
        return pltpu.make_async_copy(x_ref.at[pl.ds(r, 1)], xb_ref.at[pl.ds(dest_ref[base + 2 * r + k], 1)], sem)

    def issue(r, carry):
        row_copy(r, 0).start(priority=0)
        row_copy(r, 1).start(priority=1)
        return carry

    lax.fori_loop(0, ROW_TILE, issue, 0, unroll=DMA_UNROLL)
    for r in range(ROW_TILE):
        row_copy(r, 0).wait()
        row_copy(r, 1).wait()


def _dispatch(dest, pend, h2):
    grid_spec = pltpu.PrefetchScalarGridSpec(
        num_scalar_prefetch=2,
        grid=(N_ROW_TILES,),
        in_specs=[pl.BlockSpec((ROW_TILE, D_MODEL), lambda i, d, p: (i, 0))],
        out_specs=pl.BlockSpec(memory_space=pl.ANY),
        scratch_shapes=[pltpu.VMEM((MOE_BLOCK, D_MODEL), F32), pltpu.SemaphoreType.DMA, pltpu.SemaphoreType.DMA],
    )
    return pl.pallas_call(
        _dispatch_kernel,
        grid_spec=grid_spec,
        out_shape=jax.ShapeDtypeStruct((N_SLOTS, D_MODEL), F32),
        compiler_params=_cparams(("arbitrary",)),
        name="moe_dispatch",
    )(dest, pend, h2)


def _expert_kernel(be_ref, nx_ref, sl_ref, nu_ref, x_ref, wg_hbm, wu_hbm, wd_hbm, y_ref,
                   wg_f, wu_f, wd_f, wg_s, wu_s, wd_s, sem):
    b = pl.program_id(0)
    expert = be_ref[b]
    slot = sl_ref[b]
    used = b < nu_ref[0]
    first_of_run = jnp.logical_and(used, jnp.logical_or(b == 0, expert != be_ref[jnp.maximum(b - 1, 0)]))

    def weight_copies(e, s):
        return (pltpu.make_async_copy(wg_hbm.at[0, e], wg_f.at[s], sem.at[s]),
                pltpu.make_async_copy(wu_hbm.at[0, e], wu_f.at[s], sem.at[s]),
                pltpu.make_async_copy(wd_hbm.at[0, e], wd_f.at[s], sem.at[s]))

    @pl.when(jnp.logical_and(used, b == 0))
    def _():
        for c in weight_copies(expert, slot):
            c.start()

    @pl.when(first_of_run)
    def _():
        for c in weight_copies(expert, slot):
            c.wait()

        @pl.when(nx_ref[b] >= 0)
        def _():
            for c in weight_copies(nx_ref[b], 1 - slot):
                c.start()

        wg_s[...] = wg_f[slot].astype(BF16)
        wu_s[...] = wu_f[slot].astype(BF16)
        wd_s[...] = wd_f[slot].astype(BF16)

    @pl.when(used)
    def _():
        xb = x_ref[...].astype(BF16)
        gate = jnp.dot(xb, wg_s[...], preferred_element_type=F32)
        up = jnp.dot(xb, wu_s[...], preferred_element_type=F32)
        hh = (gate * jax.nn.sigmoid(gate)) * up
        y_ref[...] = jnp.dot(hh.astype(BF16), wd_s[...], preferred_element_type=F32)

    @pl.when(b >= nu_ref[0])
    def _():
        y_ref[...] = jnp.zeros(y_ref.shape, F32)


def _experts(block_expert, n_used, xb, w_gate, w_up, w_down):
    blocks = jnp.arange(N_BLOCKS, dtype=jnp.int32)
    starts = jnp.logical_or(blocks == 0, block_expert != jnp.roll(block_expert, 1))
    starts = jnp.logical_and(starts, blocks < n_used[0])
    run_slot = ((jnp.cumsum(starts.astype(jnp.int32)) - 1) & 1).astype(jnp.int32)
    start_idx = jnp.where(starts, blocks, N_BLOCKS)
    next_start = lax.cummin(jnp.concatenate([start_idx[1:], jnp.full((1,), N_BLOCKS, jnp.int32)]), reverse=True)
    next_expert = jnp.where(next_start < N_BLOCKS, block_expert[jnp.minimum(next_start, N_BLOCKS - 1)], -1)
    any_spec = pl.BlockSpec(memory_space=pl.ANY)
    grid_spec = pltpu.PrefetchScalarGridSpec(
        num_scalar_prefetch=4,
        grid=(N_BLOCKS,),
        in_specs=[
            pl.BlockSpec((MOE_BLOCK, D_MODEL), lambda b, be, nx, sl, nu: (jnp.minimum(b, nu[0] - 1), 0)),
            any_spec, any_spec, any_spec,
        ],
        out_specs=pl.BlockSpec((MOE_BLOCK, D_MODEL), lambda b, be, nx, sl, nu: (b, 0)),
        scratch_shapes=[
            pltpu.VMEM((2, D_MODEL, D_FF), F32),
            pltpu.VMEM((2, D_MODEL, D_FF), F32),
            pltpu.VMEM((2, D_FF, D_MODEL), F32),
            pltpu.VMEM((D_MODEL, D_FF), BF16),
            pltpu.VMEM((D_MODEL, D_FF), BF16),
            pltpu.VMEM((D_FF, D_MODEL), BF16),
            pltpu.SemaphoreType.DMA((2,)),
        ],
    )
    return pl.pallas_call(
        _expert_kernel,
        grid_spec=grid_spec,
        out_shape=jax.ShapeDtypeStruct((N_SLOTS, D_MODEL), F32),
        compiler_params=_cparams(("arbitrary",)),
        name="moe_experts",
    )(block_expert, next_expert.astype(jnp.int32), run_slot, n_used, xb, w_gate, w_up, w_down)


COMBINE_TILE0 = BLOCK // ROW_TILE
COMBINE_TILES = (SEQ + N_SAMPLE) // ROW_TILE
PROMPT_TILES = SEQ // ROW_TILE


def _combine_kernel(dest_ref, h2_ref, route_ref, g_ref, b_ref, yb_ref, yp_ref, ys_ref, buf, sem):
    i = pl.program_id(0)
    slot = lax.rem(i, 2)

    def row_copy(tile, slot_, r, k):
        src_row = dest_ref[(tile + COMBINE_TILE0) * (2 * ROW_TILE) + 2 * r + k]
        return pltpu.make_async_copy(yb_ref.at[pl.ds(src_row, 1)], buf.at[slot_, k, pl.ds(r, 1)], sem.at[slot_])

    def issue_tile(tile, slot_):
        def body(r, carry):
            row_copy(tile, slot_, r, 0).start(priority=0)
            row_copy(tile, slot_, r, 1).start(priority=1)
            return carry
        lax.fori_loop(0, ROW_TILE, body, 0, unroll=DMA_UNROLL)

    @pl.when(i == 0)
    def _():
        issue_tile(i, slot)

    for r in range(ROW_TILE):
        row_copy(i, slot, r, 0).wait()
        row_copy(i, slot, r, 1).wait()

    @pl.when(i + 1 < COMBINE_TILES)
    def _():
        issue_tile(i + 1, 1 - slot)

    route = route_ref[...]
    rows = buf[slot]
    y = route[:, 2:3] * rows[0] + route[:, 3:4] * rows[1]
    out = _layer_norm(ALPHA * h2_ref[...] + y, g_ref[...], b_ref[...])

    @pl.when(i < PROMPT_TILES)
    def _():
        yp_ref[...] = out

    @pl.when(i >= PROMPT_TILES)
    def _():
        ys_ref[...] = out


def _combine(dest, h2, route, g, b, yb):
    tile = lambda i, d: (i + COMBINE_TILE0, 0)
    grid_spec = pltpu.PrefetchScalarGridSpec(
        num_scalar_prefetch=1,
        grid=(COMBINE_TILES,),
        in_specs=[
            pl.BlockSpec((ROW_TILE, D_MODEL), tile),
            pl.BlockSpec((ROW_TILE, LANE), tile),
            pl.BlockSpec((1, D_MODEL), lambda i, d: (0, 0)),
            pl.BlockSpec((1, D_MODEL), lambda i, d: (0, 0)),
            pl.BlockSpec(memory_space=pl.ANY),
        ],
        out_specs=[
            pl.BlockSpec((ROW_TILE, D_MODEL), lambda i, d: (jnp.minimum(i, PROMPT_TILES - 1), 0)),
            pl.BlockSpec((ROW_TILE, D_MODEL), lambda i, d: (jnp.maximum(i - PROMPT_TILES, 0), 0)),
        ],
        scratch_shapes=[pltpu.VMEM((2, 2, ROW_TILE, D_MODEL), F32), pltpu.SemaphoreType.DMA((2,))],
    )
    return pl.pallas_call(
        _combine_kernel,
        grid_spec=grid_spec,
        out_shape=[jax.ShapeDtypeStruct((SEQ, D_MODEL), F32), jax.ShapeDtypeStruct((N_SAMPLE, D_MODEL), F32)],
        compiler_params=_cparams(("arbitrary",)),
        name="moe_combine",
    )(dest, h2, route, g, b, yb)


def _t5_bucket(rel):
    n = np.maximum(rel, 0)
    max_exact = N_BUCKETS // 2
    nf = np.maximum(n, max_exact).astype(np.float32)
    ratio = np.log(nf / np.float32(max_exact)) / np.float32(math.log(MAX_DISTANCE / max_exact))
    large = max_exact + (ratio * np.float32(N_BUCKETS - max_exact)).astype(np.int32)
    large = np.minimum(large, N_BUCKETS - 1)
    return np.where(n < max_exact, n, large)


def _rel_bias(rel_bias, rel):
    table = (rel_bias.astype(F32) - rel_bias.astype(F32)[N_BUCKETS - 1][None, :]) * LOG2E
    onehot = _t5_bucket(rel)[None, ..., None] == np.arange(N_BUCKETS)
    shaped = table.T.reshape((DA_HEADS,) + (1,) * rel.ndim + (N_BUCKETS,))
    return jnp.sum(jnp.where(onehot, shaped, 0.0), axis=-1)


def kernel(x_prompt, x_sample, cache_k, cache_v, state_C, state_n, state_m, page_table, meta_tokens, ln_in_g, ln_in_b, rel_bias, w_in, b_gates, lambda_q1, lambda_k1, lambda_q2, lambda_k2, diff_norm_g, mlstm_norm_g, w_out, ln_mix_g, ln_mix_b, w_router_g, b_router_g, w_router_e, b_router_e, w_gate, w_up, w_down, ln_ffn_g, ln_ffn_b):
    lam = (jnp.exp(jnp.sum(lambda_q1[0].astype(F32) * lambda_k1[0].astype(F32)))
           - jnp.exp(jnp.sum(lambda_q2[0].astype(F32) * lambda_k2[0].astype(F32))) + LAM_INIT).reshape(1)

    x = jnp.concatenate([
        jnp.zeros((META_PAD, D_MODEL), F32), meta_tokens.astype(F32), x_prompt.reshape(SEQ, D_MODEL),
        x_sample.reshape(N_SAMPLE, D_MODEL), jnp.zeros((ROWS - LP - N_SAMPLE, D_MODEL), F32)], axis=0)
    w_in_bf = w_in[0].astype(BF16)
    w_gate_cols = jnp.pad(w_in_bf[:, D_MAIN:], ((0, 0), (0, LANE - (D_IN - D_MAIN))))
    h, proj, gates, qlo, qhi, kbf, vbf = _ln_proj(
        x, ln_in_g.reshape(1, D_MODEL), ln_in_b.reshape(1, D_MODEL), w_in_bf[:, :D_MAIN], w_gate_cols)

    ar = np.arange(LANE)
    rel0 = ar[None, :] - ar[:, None]
    tiles = jnp.stack([_rel_bias(rel_bias, rel0), _rel_bias(rel_bias, rel0 + LANE)], axis=1)
    o_da = _prompt_attention(qlo, qhi, kbf, vbf, tiles, lam, jnp.zeros((ROWS, D_DIFF), F32))

    smp = slice(ROW_SAMPLE, ROW_SAMPLE + N_SAMPLE)
    q_s = proj[smp, :D_DIFF].reshape(DEC_BATCH, DEC_SEQ, DA_HEADS, 2, DA_DH) * Q_SCALE
    q5 = jnp.transpose(q_s, (0, 2, 3, 1, 4))
    eye_c = jnp.eye(2, dtype=F32)
    qt = (q5[:, :, :, :, None, :] * eye_c[None, None, :, None, :, None]).reshape(DEC_BATCH, SA_COLS, DA_DV).astype(BF16)
    zrows = jnp.zeros((DEC_BATCH, LANE - DEC_SEQ * DA_HEADS, DA_DV), F32)
    knew = jnp.concatenate([proj[smp, COL_K:COL_K + D_DIFF].reshape(DEC_BATCH, DEC_SEQ * DA_HEADS, DA_DV), zrows], axis=1)
    vnew = jnp.concatenate([proj[smp, COL_V:COL_V + D_DIFF].reshape(DEC_BATCH, DEC_SEQ * DA_HEADS, DA_DV), zrows], axis=1)
    qpos = np.arange(DEC_SEQ)
    tpos = np.arange(PAGE_SIZE)
    same_head = jnp.eye(DA_HEADS, dtype=bool)

    def score_bias(b_hqt, ok_qt):
        n_t = b_hqt.shape[-1]
        full = jnp.where(same_head[:, None, None, None, :] & ok_qt[None, None, :, :, None],
                         b_hqt[:, None, :, :, None], NEG)
        return jnp.broadcast_to(full, (DA_HEADS, 2, DEC_SEQ, n_t, DA_HEADS)).reshape(SA_COLS, n_t * DA_HEADS)

    all_ok = jnp.ones((DEC_SEQ, PAGE_SIZE), bool)
    b_last = _rel_bias(rel_bias, PAGE_SIZE + qpos[:, None] - tpos[None, :])
    bm = jnp.stack([score_bias(jnp.zeros_like(b_last), all_ok), score_bias(b_last, all_ok)])
    rel_new = qpos[:, None] - qpos[None, :]
    bmnew = jnp.concatenate([score_bias(_rel_bias(rel_bias, rel_new), rel_new >= 0),
                             jnp.full((SA_COLS, LANE - DEC_SEQ * DA_HEADS), NEG, F32)], axis=1)
    bmnew = jnp.stack([jnp.full((SA_COLS, LANE), NEG, F32), bmnew])
    o_da = _sample_attention(qt, knew, vnew, bm, bmnew, cache_k, cache_v, page_table.astype(jnp.int32), lam, o_da)

    gate_bias = jnp.pad(b_gates[0].astype(F32), (0, LANE - 2 * ML_HEADS)).reshape(1, LANE)
    h_ml, c_p, n_p, m_p = _mlstm(
        proj, gates, gate_bias, jnp.zeros((1, ML_HEADS, ML_DQK, ML_DV), F32),
        jnp.zeros((1, ML_HEADS, ML_DQK), F32), jnp.zeros((1, ML_HEADS, LANE), F32),
        jnp.zeros((ROWS, D_MLSTM), F32), n_seq=1, rows=BLOCK, carry=True, row_block0=0,
        n_steps=N_CHUNKS, pad_front=META_PAD)
    h_ml, c_s, n_s, m_s = _mlstm(
        proj, gates, gate_bias, state_C[0].astype(F32), state_n[0].astype(F32),
        jnp.broadcast_to(state_m[0].astype(F32)[:, :, None], (DEC_BATCH, ML_HEADS, LANE)), h_ml,
        n_seq=DEC_BATCH, rows=DEC_SEQ, carry=False, row_block0=ROW_SAMPLE // DEC_SEQ,
        n_steps=DEC_BATCH, pad_front=0)

    wr = jnp.pad(jnp.concatenate([w_router_g[0], w_router_e[0]], axis=1).astype(F32),
                 ((0, 0), (0, LANE - N_GROUPS - N_EXPERTS)))
    wr_hi = wr.astype(BF16)
    wr_lo = (wr - wr_hi.astype(F32)).astype(BF16)
    br = jnp.pad(jnp.concatenate([b_router_g[0], b_router_e[0]]).astype(F32),
                 (0, LANE - N_GROUPS - N_EXPERTS)).reshape(1, LANE)
    h2, route, counts = _mix_router(
        o_da, h_ml, proj, h, w_out[0].astype(BF16), diff_norm_g[0].reshape(1, DA_DV),
        mlstm_norm_g[0].reshape(1, D_MLSTM), ln_mix_g[0].reshape(1, D_MODEL), ln_mix_b[0].reshape(1, D_MODEL),
        wr_hi, wr_lo, br)

    cnt = counts[0, :N_EXPERTS].astype(jnp.int32)
    padded = (cnt + MOE_BLOCK - 1) // MOE_BLOCK * MOE_BLOCK
    pend = jnp.cumsum(padded)
    pstart = pend - padded
    e12 = route[:, 0:2].astype(jnp.int32)
    experts = jnp.arange(N_EXPERTS, dtype=jnp.int32)
    slot0 = jnp.sum(jnp.where(e12[:, :, None] == experts, pstart, 0), axis=-1)
    dest = (slot0 + route[:, 4:6].astype(jnp.int32)).reshape(N_ASSIGN)
    block_start = jnp.arange(N_BLOCKS, dtype=jnp.int32) * MOE_BLOCK
    block_expert = jnp.minimum(jnp.sum(block_start[:, None] >= pend[None, :], axis=1), N_EXPERTS - 1).astype(jnp.int32)
    n_used = (pend[-1] // MOE_BLOCK).astype(jnp.int32).reshape(1)

    xb = _dispatch(dest, pend.astype(jnp.int32), h2)
    yb = _experts(block_expert, n_used, xb, w_gate, w_up, w_down)
    y_p, y_s = _combine(dest, h2, route, ln_ffn_g[0].reshape(1, D_MODEL), ln_ffn_b[0].reshape(1, D_MODEL), yb)

    lq = N_META + SEQ
    y_prompt = y_p.reshape(1, SEQ, D_MODEL)
    y_sample = y_s.reshape(DEC_BATCH, DEC_SEQ, D_MODEL)
    k_prompt = proj[META_PAD:LP, COL_K:COL_K + D_DIFF].reshape(1, 1, lq, DA_HEADS, 2 * DA_DH)
    v_prompt = proj[META_PAD:LP, COL_V:COL_V + D_DIFF].reshape(1, 1, lq, DA_HEADS, DA_DV)
    k_sample = proj[smp, COL_K:COL_K + D_DIFF].reshape(1, DEC_BATCH, DEC_SEQ, DA_HEADS, 2 * DA_DH)
    v_sample = proj[smp, COL_V:COL_V + D_DIFF].reshape(1, DEC_BATCH, DEC_SEQ, DA_HEADS, DA_DV)
    return (y_prompt, y_sample, k_prompt, v_prompt,
            c_p[None], n_p[None], m_p[None, :, :, 0],
            k_sample, v_sample, c_s[None], n_s[None], m_s[None, :, :, 0])
```
